```python
import math
import jax, jax.numpy as jnp
from jax import lax
import numpy as np

D_MODEL = 1024
BATCH = 4
SEQ = 8192
DEPTH = 1

CTX_LEN = 256
GRID_W = 64
MIX_WIDTH = D_MODEL
FOURIER_WIDTH = MIX_WIDTH // 2
N_FOURIER_GROUPS = 4
FOURIER_GROUP_DIM = FOURIER_WIDTH // N_FOURIER_GROUPS
ATTN_WIDTH = MIX_WIDTH - FOURIER_WIDTH
HEAD_DIM = 64
N_HEADS = ATTN_WIDTH // HEAD_DIM
N_KV_HEADS = 2
Q_PER_KV = N_HEADS // N_KV_HEADS
KV_WIDTH = N_KV_HEADS * HEAD_DIM
IN_WIDTH = FOURIER_WIDTH + ATTN_WIDTH + 2 * KV_WIDTH
D_FF = ((math.ceil(8 * D_MODEL / 3) + 255) // 256) * 256
ROPE_THETA = 10000.0
Q_BLOCK = 128
EPS = 1e-6

kernel_name = "hybrid_fourier_gqa_dit_block"


def rmsnorm(x, g):
    xf = x.astype(jnp.float32)
    y = xf * lax.rsqrt(jnp.mean(xf * xf, axis=-1, keepdims=True) + EPS)
    return (y * g.astype(jnp.float32)).astype(x.dtype)


def adaln(cond, w_ada, b_ada):
    return jnp.split(jax.nn.silu(cond) @ w_ada + b_ada, 6, axis=-1)


def modulate(h, shift, scale):
    return h * (1 + scale) + shift


def axial_angles(n_tokens):
    rows = n_tokens // GRID_W
    row_ids = jnp.repeat(jnp.arange(rows), GRID_W, total_repeat_length=n_tokens)
    col_ids = jnp.tile(jnp.arange(GRID_W), rows)
    n_freq = HEAD_DIM // 4
    inv_freq = ROPE_THETA ** (-jnp.arange(n_freq, dtype=jnp.float32) / n_freq)
    row_ang = row_ids.astype(jnp.float32)[:, None] * inv_freq
    col_ang = col_ids.astype(jnp.float32)[:, None] * inv_freq
    return row_ang, col_ang


def rotate(xp, ang):
    x1, x2 = jnp.split(xp, 2, axis=-1)
    cos = jnp.cos(ang).astype(xp.dtype)
    sin = jnp.sin(ang).astype(xp.dtype)
    return jnp.concatenate([x1 * cos - x2 * sin, x2 * cos + x1 * sin], axis=-1)


def apply_axial_rope(x, row_ang, col_ang):
    xr, xc = jnp.split(x, 2, axis=-1)
    return jnp.concatenate([rotate(xr, row_ang[:, None, :]),
                            rotate(xc, col_ang[:, None, :])], axis=-1)


def split_groups(p):
    b, l, _ = p.shape
    o1 = FOURIER_WIDTH
    o2 = o1 + ATTN_WIDTH
    o3 = o2 + KV_WIDTH
    u = p[..., :o1]
    q = p[..., o1:o2].reshape(b, l, N_HEADS, HEAD_DIM)
    k = p[..., o2:o3].reshape(b, l, N_KV_HEADS, HEAD_DIM)
    v = p[..., o3:].reshape(b, l, N_KV_HEADS, HEAD_DIM)
    return u, q, k, v


def fourier_mix(u, w_four):
    b, l, _ = u.shape
    ug = u.reshape(b, l, N_FOURIER_GROUPS, FOURIER_GROUP_DIM).astype(jnp.float32)
    y = jnp.fft.fft2(ug, axes=(1, 3), norm="ortho").real.astype(u.dtype)
    y = jnp.einsum('blgc,gcd->blgd', y, w_four)
    return y.reshape(b, l, FOURIER_WIDTH)


def attend_block(qb, k, v):
    s = jnp.einsum('bkgqd,bknd->bkgqn', qb, k) * (HEAD_DIM ** -0.5)
    p = jax.nn.softmax(s.astype(jnp.float32), axis=-1).astype(v.dtype)
    return jnp.einsum('bkgqn,bknd->bkgqd', p, v)


def group_queries(q):
    b, l, _, _ = q.shape
    return q.reshape(b, l, N_KV_HEADS, Q_PER_KV, HEAD_DIM).transpose(0, 2, 3, 1, 4)


def ungroup(o):
    b, _, _, l, _ = o.shape
    return o.transpose(0, 3, 1, 2, 4).reshape(b, l, ATTN_WIDTH)


def latent_attention(q, k_all, v_all):
    b, s = q.shape[0], q.shape[1]
    n_blk = s // Q_BLOCK
    qg = group_queries(q).reshape(b, N_KV_HEADS, Q_PER_KV, n_blk, Q_BLOCK, HEAD_DIM)
    qg = jnp.moveaxis(qg, 3, 0)
    out = lax.map(lambda qb: attend_block(qb, k_all, v_all), qg)
    out = jnp.moveaxis(out, 0, 3).reshape(b, N_KV_HEADS, Q_PER_KV, s, HEAD_DIM)
    return ungroup(out)


def swiglu(h, w_gate, w_up, w_down):
    return (jax.nn.silu(h @ w_gate) * (h @ w_up)) @ w_down


def setup_inputs(seed: int = 0) -> dict:
    key = jax.random.key(seed)
    ks = jax.random.split(key, 20)
    f32 = jnp.float32
    nrm = lambda k, shape, s: jax.random.normal(k, shape, f32) * s
    return {
        "x": nrm(ks[0], (BATCH, SEQ, D_MODEL), 1.0),
        "c": nrm(ks[1], (BATCH, D_MODEL), 1.0),
        "ctx": nrm(ks[2], (BATCH, CTX_LEN, D_MODEL), 1.0),
        "c_ctx": nrm(ks[3], (D_MODEL,), 1.0),
        "w_ada": nrm(ks[4], (DEPTH, D_MODEL, 6 * D_MODEL), 0.5 * D_MODEL ** -0.5),
        "b_ada": nrm(ks[5], (DEPTH, 6 * D_MODEL), 0.01),
        "g_mix": 1.0 + nrm(ks[6], (DEPTH, D_MODEL), 0.05),
        "w_in": nrm(ks[7], (DEPTH, D_MODEL, IN_WIDTH), D_MODEL ** -0.5),
        "w_four": nrm(ks[8], (DEPTH, N_FOURIER_GROUPS, FOURIER_GROUP_DIM, FOURIER_GROUP_DIM),
                      FOURIER_GROUP_DIM ** -0.5),
        "q_gain": 1.0 + nrm(ks[9], (DEPTH, HEAD_DIM), 0.05),
        "k_gain": 1.0 + nrm(ks[10], (DEPTH, HEAD_DIM), 0.05),
        "w_out": nrm(ks[11], (DEPTH, MIX_WIDTH, D_MODEL), MIX_WIDTH ** -0.5),
        "g_ffn": 1.0 + nrm(ks[12], (DEPTH, D_MODEL), 0.05),
        "w_gate": nrm(ks[13], (DEPTH, D_MODEL, D_FF), D_MODEL ** -0.5),
        "w_up": nrm(ks[14], (DEPTH, D_MODEL, D_FF), D_MODEL ** -0.5),
        "w_down": nrm(ks[15], (DEPTH, D_FF, D_MODEL), D_FF ** -0.5),
        "g_final": 1.0 + nrm(ks[16], (D_MODEL,), 0.05),
    }


def reference(x, c, ctx, c_ctx, w_ada, b_ada, g_mix, w_in, w_four, q_gain, k_gain,
              w_out, g_ffn, w_gate, w_up, w_down, g_final):
    n_lat = x.shape[1]
    row_ang, col_ang = axial_angles(n_lat)
    xc = ctx
    for l in range(DEPTH):
        last = l == DEPTH - 1
        sh1, sc1, gt1, sh2, sc2, gt2 = [m[:, None, :] for m in adaln(c, w_ada[l], b_ada[l])]
        csh1, csc1, cgt1, csh2, csc2, cgt2 = adaln(c_ctx, w_ada[l], b_ada[l])

        h = modulate(rmsnorm(x, g_mix[l]), sh1, sc1)
        hc = modulate(rmsnorm(xc, g_mix[l]), csh1, csc1)
        u, q, k, v = split_groups(h @ w_in[l])
        uc, qc, kc, vc = split_groups(hc @ w_in[l])

        q = apply_axial_rope(rmsnorm(q, q_gain[l]), row_ang, col_ang)
        k = apply_axial_rope(rmsnorm(k, k_gain[l]), row_ang, col_ang)
        kc = rmsnorm(kc, k_gain[l])
        k_ctx = kc.transpose(0, 2, 1, 3)
        v_ctx = vc.transpose(0, 2, 1, 3)
        k_all = jnp.concatenate([k.transpose(0, 2, 1, 3), k_ctx], axis=2)
        v_all = jnp.concatenate([v.transpose(0, 2, 1, 3), v_ctx], axis=2)

        mix = jnp.concatenate([fourier_mix(u, w_four[l]),
                               latent_attention(q, k_all, v_all)], axis=-1) @ w_out[l]
        x = x + gt1 * mix
        h2 = modulate(rmsnorm(x, g_ffn[l]), sh2, sc2)
        x = x + gt2 * swiglu(h2, w_gate[l], w_up[l], w_down[l])

        if not last:
            qc = rmsnorm(qc, q_gain[l])
            attn_c = ungroup(attend_block(group_queries(qc), k_ctx, v_ctx))
            mix_c = jnp.concatenate([fourier_mix(uc, w_four[l]), attn_c], axis=-1) @ w_out[l]
            xc = xc + cgt1 * mix_c
            hc2 = modulate(rmsnorm(xc, g_ffn[l]), csh2, csc2)
            xc = xc + cgt2 * swiglu(hc2, w_gate[l], w_up[l], w_down[l])
    return rmsnorm(x, g_final)
```

```python
import functools
import math

import numpy as np
import jax
import jax.numpy as jnp
from jax import lax
from jax.experimental import pallas as pl
from jax.experimental.pallas import tpu as pltpu

F32 = jnp.float32
BF16 = jnp.bfloat16

EPS = 1e-6
ROPE_THETA = 10000.0
GRID_W = 64
HEAD_DIM = 64
N_HEADS = 8
N_KV_HEADS = 2
Q_PER_KV = N_HEADS // N_KV_HEADS
N_FOURIER_GROUPS = 4
FOURIER_GROUP_DIM = 128
FOURIER_WIDTH = N_FOURIER_GROUPS * FOURIER_GROUP_DIM
ATTN_WIDTH = N_HEADS * HEAD_DIM
KV_WIDTH = N_KV_HEADS * HEAD_DIM

LANES = 128
SUBLANES = 8
DFT_INNER = 128
ROW_PITCH = DFT_INNER + SUBLANES
VMEM_LIMIT = 56 * 1024 * 1024

Q_SCALE = (HEAD_DIM ** -0.5) * math.log2(math.e)


def _dot(a, b):
    return jnp.dot(a, b, preferred_element_type=F32)


def _rms(x):
    return x * lax.rsqrt(jnp.mean(x * x, axis=-1, keepdims=True) + EPS)


def _adaln_kernel(cond_ref, w_ref, b_ref, o_ref):
    c = cond_ref[...]
    s = c / (1.0 + jnp.exp(-c))
    o_ref[...] = jnp.dot(s, w_ref[...], preferred_element_type=F32,
                         precision=lax.Precision.HIGHEST) + b_ref[...]


def _adaln(cond, w, b):
    rows, d = cond.shape
    n = w.shape[1]
    tn = n // 4
    return pl.pallas_call(
        _adaln_kernel,
        grid=(n // tn,),
        in_specs=[pl.BlockSpec((rows, d), lambda j: (0, 0)),
                  pl.BlockSpec((d, tn), lambda j: (0, j)),
                  pl.BlockSpec((1, tn), lambda j: (0, j))],
        out_specs=pl.BlockSpec((rows, tn), lambda j: (0, j)),
        out_shape=jax.ShapeDtypeStruct((rows, n), F32),
        compiler_params=pltpu.CompilerParams(dimension_semantics=("arbitrary",),
                                             vmem_limit_bytes=VMEM_LIMIT),
        name="adaln",
    )(cond, w, b)


def _proj_kernel(x_ref, sh_ref, sc_ref, g_ref, w_ref, qg_ref, kg_ref, bd_ref, cos_ref, sin_ref,
                 u_ref, q_ref, kt_ref, v_ref):
    tm = x_ref.shape[0]
    h = _rms(x_ref[...]) * g_ref[...]
    h = h * (1.0 + sc_ref[...]) + sh_ref[...]
    p = _dot(h.astype(BF16), w_ref[...])
    o1 = FOURIER_WIDTH
    o2 = o1 + ATTN_WIDTH
    o3 = o2 + KV_WIDTH
    u_ref[...] = p[:, :o1].astype(u_ref.dtype)
    v_ref[...] = p[:, o3:].astype(v_ref.dtype)

    cos = cos_ref[...]
    sin = sin_ref[...]
    lane = lax.broadcasted_iota(jnp.int32, (tm, LANES), 1)
    first_half = (lane & 31) < 16

    def norm_rope(t, gain, bd):
        t2 = t * t
        hi = t2.astype(BF16)
        lo = (t2 - hi.astype(F32)).astype(BF16)
        ss = _dot(hi, bd) + _dot(lo, bd)
        tn = t * lax.rsqrt(ss * (1.0 / HEAD_DIM) + EPS) * gain
        outs = []
        for j in range(t.shape[1] // LANES):
            tj = tn[:, j * LANES:(j + 1) * LANES]
            partner = jnp.where(first_half,
                                pltpu.roll(tj, LANES - 16, axis=1),
                                pltpu.roll(tj, 16, axis=1))
            outs.append(tj * cos + partner * sin)
        return outs

    q_slabs = norm_rope(p[:, o1:o2], qg_ref[...], bd_ref[...])
    for j, qs in enumerate(q_slabs):
        q_ref[:, j * LANES:(j + 1) * LANES] = (qs * Q_SCALE).astype(q_ref.dtype)
    (k_rot,) = norm_rope(p[:, o2:o3], kg_ref[...], bd_ref[:KV_WIDTH, :KV_WIDTH])
    kt_ref[...] = k_rot.T.astype(kt_ref.dtype)


def _project(x, shift, scale, gain, w_in, q_gain, k_gain, bd, cos, sin, tm):
    b, s, d = x.shape
    in_w = w_in.shape[1]
    vec = pl.BlockSpec((None, 1, d), lambda bi, i: (bi, 0, 0))
    const = lambda shape: pl.BlockSpec(shape, lambda bi, i: (0,) * len(shape))
    return pl.pallas_call(
        _proj_kernel,
        grid=(b, s // tm),
        in_specs=[pl.BlockSpec((None, tm, d), lambda bi, i: (bi, i, 0)),
                  vec, vec, const((1, d)), const((d, in_w)),
                  const((1, ATTN_WIDTH)), const((1, KV_WIDTH)), const((ATTN_WIDTH, ATTN_WIDTH)),
                  pl.BlockSpec((tm, LANES), lambda bi, i: (i, 0)),
                  pl.BlockSpec((tm, LANES), lambda bi, i: (i, 0))],
        out_specs=[pl.BlockSpec((None, tm, FOURIER_WIDTH), lambda bi, i: (bi, i, 0)),
                   pl.BlockSpec((None, tm, ATTN_WIDTH), lambda bi, i: (bi, i, 0)),
                   pl.BlockSpec((None, KV_WIDTH, tm), lambda bi, i: (bi, 0, i)),
                   pl.BlockSpec((None, tm, KV_WIDTH), lambda bi, i: (bi, i, 0))],
        out_shape=[jax.ShapeDtypeStruct((b, s, FOURIER_WIDTH), BF16),
                   jax.ShapeDtypeStruct((b, s, ATTN_WIDTH), BF16),
                   jax.ShapeDtypeStruct((b, KV_WIDTH, s), BF16),
                   jax.ShapeDtypeStruct((b, s, KV_WIDTH), BF16)],
        compiler_params=pltpu.CompilerParams(dimension_semantics=("parallel", "parallel"),
                                             vmem_limit_bytes=VMEM_LIMIT),
        name="proj",
    )(x, shift, scale, gain, w_in, q_gain, k_gain, bd, cos, sin)


def _fourier_kernel(u_ref, ma_ref, mb_ref, mc_ref, wf_ref, o_ref, usc, tsc):
    n2_count = DFT_INNER
    n1_count = u_ref.shape[0] // n2_count

    def fill(i, carry):
        src = pl.multiple_of(i * n2_count, n2_count)
        dst = pl.multiple_of(i * ROW_PITCH, SUBLANES)
        usc[pl.ds(dst, n2_count), :] = u_ref[pl.ds(src, n2_count), :].astype(F32)
        return carry
    lax.fori_loop(0, n1_count, fill, 0)

    t_pitch = 2 * n1_count + SUBLANES

    def stage_a(n2, carry):
        un = usc[pl.ds(n2, n1_count, stride=ROW_PITCH), :].astype(BF16)
        t = _dot(ma_ref[n2], un)
        dst = pl.multiple_of(n2 * t_pitch, SUBLANES)
        tsc[pl.ds(dst, 2 * n1_count), :] = t
        return carry
    lax.fori_loop(0, n2_count, stage_a, 0)

    def stage_b(k1, carry):
        tr = tsc[pl.ds(k1, n2_count, stride=t_pitch), :]
        ti = tsc[pl.ds(n1_count + k1, n2_count, stride=t_pitch), :]
        t2 = jnp.concatenate([tr, ti], axis=0).astype(BF16)
        v = _dot(mb_ref[...], t2)
        vcat = jnp.concatenate([v[:n2_count], v[n2_count:]], axis=1).astype(BF16)
        y = _dot(vcat, mc_ref[...]).astype(BF16)
        o_ref[pl.ds(k1, n2_count, stride=n1_count), :] = _dot(y, wf_ref[...])
        return carry
    lax.fori_loop(0, n1_count, stage_b, 0)


def _fourier(u, ma, mb, mc, w_four):
    b, s, _ = u.shape
    c = FOURIER_GROUP_DIM
    n1 = s // DFT_INNER
    return pl.pallas_call(
        _fourier_kernel,
        grid=(b, N_FOURIER_GROUPS),
        in_specs=[pl.BlockSpec((None, s, c), lambda bi, g: (bi, 0, g)),
                  pl.BlockSpec(ma.shape, lambda bi, g: (0, 0, 0)),
                  pl.BlockSpec(mb.shape, lambda bi, g: (0, 0)),
                  pl.BlockSpec(mc.shape, lambda bi, g: (0, 0)),
                  pl.BlockSpec((None, c, c), lambda bi, g: (g, 0, 0))],
        out_specs=pl.BlockSpec((None, s, c), lambda bi, g: (bi, 0, g)),
        out_shape=jax.ShapeDtypeStruct((b, s, FOURIER_WIDTH), F32),
        scratch_shapes=[pltpu.VMEM((n1 * ROW_PITCH, c), F32),
                        pltpu.VMEM((DFT_INNER * (2 * n1 + SUBLANES), c), F32)],
        compiler_params=pltpu.CompilerParams(dimension_semantics=("parallel", "parallel"),
                                             vmem_limit_bytes=VMEM_LIMIT),
        name="fourier",
    )(u, ma, mb, mc, w_four)


def _fourier_tables(s):
    n2c = DFT_INNER
    n1c = s // n2c
    k1 = np.arange(n1c, dtype=np.float64)
    n = (n2c * np.arange(n1c)[None, :] + np.arange(n2c)[:, None]).astype(np.float64)
    ang = 2.0 * np.pi * k1[None, :, None] * n[:, None, :] / s
    ma = np.concatenate([np.cos(ang), -np.sin(ang)], axis=1)
    kk = np.arange(n2c, dtype=np.float64)
    a2 = 2.0 * np.pi * np.outer(kk, kk) / n2c
    c2, s2 = np.cos(a2), np.sin(a2)
    mb = np.block([[c2, s2], [-s2, c2]])
    cc = np.arange(FOURIER_GROUP_DIM, dtype=np.float64)
    ac = 2.0 * np.pi * np.outer(cc, cc) / FOURIER_GROUP_DIM
    norm = 1.0 / math.sqrt(s * FOURIER_GROUP_DIM)
    mc = np.concatenate([np.cos(ac), np.sin(ac)], axis=0) * norm
    return (ma.astype(np.float32), mb.astype(np.float32), mc.astype(np.float32))


def _attn_kernel(q_ref, kt_ref, v_ref, ktc_ref, vc_ref, o_ref, m_ref, l_ref, acc_ref, *, tk):
    tq = q_ref.shape[0]
    n_lat = kt_ref.shape[1]
    g = pl.program_id(1)
    qs = jnp.concatenate([q_ref[:, h * HEAD_DIM:(h + 1) * HEAD_DIM] for h in range(Q_PER_KV)],
                         axis=0)
    m_ref[...] = jnp.full(m_ref.shape, -jnp.inf, F32)
    l_ref[...] = jnp.zeros(l_ref.shape, F32)
    acc_ref[...] = jnp.zeros(acc_ref.shape, F32)

    def update(kt, v):
        s = _dot(qs, kt)
        m_prev = m_ref[...]
        m_next = jnp.maximum(m_prev, jnp.max(s, axis=1, keepdims=True))
        alpha = jnp.exp2(m_prev - m_next)
        p = jnp.exp2(s - jnp.tile(m_next, (1, s.shape[1] // LANES)))
        l_ref[...] = alpha * l_ref[...] + jnp.sum(p, axis=1, keepdims=True)
        acc_ref[...] = alpha * acc_ref[...] + _dot(p.astype(BF16), v)
        m_ref[...] = m_next

    def body(j, carry):
        start = pl.multiple_of(j * tk, tk)
        update(kt_ref[:, pl.ds(start, tk)], v_ref[pl.ds(start, tk), :])
        return carry
    lax.fori_loop(0, n_lat // tk, body, 0)
    update(ktc_ref[...], vc_ref[...])

    out = acc_ref[...] / l_ref[...]
    out = jnp.where(g == 0, out[:, :HEAD_DIM], out[:, HEAD_DIM:])
    o_ref[...] = jnp.concatenate([out[h * tq:(h + 1) * tq] for h in range(Q_PER_KV)],
                                 axis=1).astype(o_ref.dtype)


def _attention(q, kt, v, ktc, vc, tq, tk):
    b, s, _ = q.shape
    ctx = ktc.shape[2]
    gw = Q_PER_KV * HEAD_DIM
    rows = Q_PER_KV * tq
    return pl.pallas_call(
        functools.partial(_attn_kernel, tk=tk),
        grid=(b, N_KV_HEADS, s // tq),
        in_specs=[pl.BlockSpec((None, tq, gw), lambda bi, g, i: (bi, i, g)),
                  pl.BlockSpec((None, HEAD_DIM, s), lambda bi, g, i: (bi, g, 0)),
                  pl.BlockSpec((None, s, KV_WIDTH), lambda bi, g, i: (bi, 0, 0)),
                  pl.BlockSpec((None, HEAD_DIM, ctx), lambda bi, g, i: (bi, g, 0)),
                  pl.BlockSpec((None, ctx, KV_WIDTH), lambda bi, g, i: (bi, 0, 0))],
        out_specs=pl.BlockSpec((None, tq, gw), lambda bi, g, i: (bi, i, g)),
        out_shape=jax.ShapeDtypeStruct((b, s, ATTN_WIDTH), BF16),
        scratch_shapes=[pltpu.VMEM((rows, LANES), F32),
                        pltpu.VMEM((rows, LANES), F32),
                        pltpu.VMEM((rows, KV_WIDTH), F32)],
        compiler_params=pltpu.CompilerParams(
            dimension_semantics=("parallel", "parallel", "parallel"),
            vmem_limit_bytes=VMEM_LIMIT),
        name="attention",
    )(q, kt, v, ktc, vc)


def _ffn_kernel(x_ref, fm_ref, at_ref, gt1_ref, sh2_ref, sc2_ref, gt2_ref, gffn_ref, gfin_ref,
                wo_ref, wg_ref, wu_ref, wd_ref, o_ref, *, ff_chunk):
    mix = (_dot(fm_ref[...].astype(BF16), wo_ref[:FOURIER_WIDTH, :])
           + _dot(at_ref[...], wo_ref[FOURIER_WIDTH:, :]))
    x1 = x_ref[...] + gt1_ref[...] * mix
    h2 = (_rms(x1) * gffn_ref[...] * (1.0 + sc2_ref[...]) + sh2_ref[...]).astype(BF16)
    d_ff = wg_ref.shape[1]
    ffn = None
    for c0 in range(0, d_ff, ff_chunk):
        gate = _dot(h2, wg_ref[:, c0:c0 + ff_chunk])
        up = _dot(h2, wu_ref[:, c0:c0 + ff_chunk])
        act = (gate / (1.0 + jnp.exp(-gate)) * up).astype(BF16)
        part = _dot(act, wd_ref[c0:c0 + ff_chunk, :])
        ffn = part if ffn is None else ffn + part
    x2 = x1 + gt2_ref[...] * ffn
    o_ref[...] = _rms(x2) * gfin_ref[...]


def _ffn(x, fm, at, gt1, sh2, sc2, gt2, g_ffn, g_final, w_out, w_gate, w_up, w_down, tm, ff_chunk):
    b, s, d = x.shape
    d_ff = w_gate.shape[1]
    vec = pl.BlockSpec((None, 1, d), lambda bi, i: (bi, 0, 0))
    single = pl.Buffered(1)
    const = lambda shape: pl.BlockSpec(shape, lambda bi, i: (0,) * len(shape))
    weight = lambda shape: pl.BlockSpec(shape, lambda bi, i: (0,) * len(shape), pipeline_mode=single)
    return pl.pallas_call(
        functools.partial(_ffn_kernel, ff_chunk=ff_chunk),
        grid=(b, s // tm),
        in_specs=[pl.BlockSpec((None, tm, d), lambda bi, i: (bi, i, 0)),
                  pl.BlockSpec((None, tm, FOURIER_WIDTH), lambda bi, i: (bi, i, 0)),
                  pl.BlockSpec((None, tm, ATTN_WIDTH), lambda bi, i: (bi, i, 0)),
                  vec, vec, vec, vec, const((1, d)), const((1, d)),
                  weight((d, d)), weight((d, d_ff)), weight((d, d_ff)), weight((d_ff, d))],
        out_specs=pl.BlockSpec((None, tm, d), lambda bi, i: (bi, i, 0)),
        out_shape=jax.ShapeDtypeStruct((b, s, d), F32),
        compiler_params=pltpu.CompilerParams(dimension_semantics=("parallel", "parallel"),
                                             vmem_limit_bytes=VMEM_LIMIT),
        name="ffn",
    )(x, fm, at, gt1, sh2, sc2, gt2, g_ffn, g_final, w_out, w_gate, w_up, w_down)


def _rope_tables(n_tokens):
    n_freq = HEAD_DIM // 4
    inv_freq = ROPE_THETA ** (-np.arange(n_freq, dtype=np.float64) / n_freq)
    t = np.arange(n_tokens)
    row_ang = (t // GRID_W).astype(np.float64)[:, None] * inv_freq[None, :]
    col_ang = (t % GRID_W).astype(np.float64)[:, None] * inv_freq[None, :]
    ang = np.concatenate([row_ang, row_ang, col_ang, col_ang], axis=1)
    sign = np.tile(np.concatenate([-np.ones(n_freq), np.ones(n_freq)]), 2)
    cos = np.cos(ang)
    sin = np.sin(ang) * sign[None, :]
    reps = LANES // HEAD_DIM
    return (np.tile(cos, (1, reps)).astype(np.float32), np.tile(sin, (1, reps)).astype(np.float32))


def _head_block_diag(width):
    idx = np.arange(width) // HEAD_DIM
    return (idx[:, None] == idx[None, :]).astype(np.float32)


def kernel(x, c, ctx, c_ctx, w_ada, b_ada, g_mix, w_in, w_four, q_gain, k_gain, w_out, g_ffn,
           w_gate, w_up, w_down, g_final):
    b, s, d = x.shape
    n_ctx = ctx.shape[1]
    assert w_ada.shape[0] == 1, "single-layer block"
    assert s % (DFT_INNER * SUBLANES) == 0 and n_ctx % LANES == 0

    pad = (-(b + 1)) % SUBLANES
    cond = jnp.concatenate([c, c_ctx[None, :], jnp.zeros((pad, d), F32)], axis=0)
    ada = _adaln(cond, w_ada[0], b_ada[0][None, :])
    sh1, sc1, gt1, sh2, sc2, gt2 = [ada[:, i * d:(i + 1) * d] for i in range(6)]
    per_batch = lambda m: m[:b, None, :]
    for_ctx = lambda m: jnp.broadcast_to(m[b][None, None, :], (b, 1, d))

    w_in_b = w_in[0].astype(BF16)
    bd = jnp.asarray(_head_block_diag(ATTN_WIDTH)).astype(BF16)
    qg = jnp.tile(q_gain[0], N_HEADS)[None, :]
    kg = jnp.tile(k_gain[0], N_KV_HEADS)[None, :]
    g_mix2 = g_mix[0][None, :]

    cos, sin = _rope_tables(s)
    tm_proj = min(512, s)
    u, q, kt, v = _project(x, per_batch(sh1), per_batch(sc1), g_mix2, w_in_b, qg, kg, bd,
                           jnp.asarray(cos), jnp.asarray(sin), tm_proj)
    _, _, ktc, vc = _project(ctx, for_ctx(sh1), for_ctx(sc1), g_mix2, w_in_b, qg, kg, bd,
                             jnp.ones((n_ctx, LANES), F32), jnp.zeros((n_ctx, LANES), F32), n_ctx)

    ma, mb, mc = _fourier_tables(s)
    fm = _fourier(u, jnp.asarray(ma).astype(BF16), jnp.asarray(mb).astype(BF16),
                  jnp.asarray(mc).astype(BF16), w_four[0].astype(BF16))

    at = _attention(q, kt, v, ktc, vc, tq=min(256, s), tk=min(512, s))

    tm_ffn = min(512, s)
    d_ff = w_gate.shape[2]
    ff_chunk = d_ff // 2 if (d_ff // 2) % LANES == 0 else d_ff
    return _ffn(x, fm, at, per_batch(gt1), per_batch(sh2), per_batch(sc2), per_batch(gt2),
                g_ffn[0][None, :], g_final[None, :], w_out[0].astype(BF16), w_gate[0].astype(BF16),
                w_up[0].astype(BF16), w_down[0].astype(BF16), tm_ffn, ff_chunk)
```

```python
import functools
import math

import numpy as np
import jax
import jax.numpy as jnp
from jax import lax
from jax.experimental import pallas as pl
from jax.experimental.pallas import tpu as pltpu

F32 = jnp.float32
BF16 = jnp.bfloat16

EPS = 1e-6
ROPE_THETA = 10000.0
GRID_W = 64
HEAD_DIM = 64
N_HEADS = 8
N_KV_HEADS = 2
Q_PER_KV = N_HEADS // N_KV_HEADS
N_FOURIER_GROUPS = 4
FOURIER_GROUP_DIM = 128
FOURIER_WIDTH = N_FOURIER_GROUPS * FOURIER_GROUP_DIM
ATTN_WIDTH = N_HEADS * HEAD_DIM
KV_WIDTH = N_KV_HEADS * HEAD_DIM

LANES = 128
SUBLANES = 8
DFT_INNER = 128
ROW_PITCH = DFT_INNER + SUBLANES
VMEM_LIMIT = 56 * 1024 * 1024

Q_SCALE = (HEAD_DIM ** -0.5) * math.log2(math.e)
BF16_SUBLANES = 16
VT_ROWS = HEAD_DIM + BF16_SUBLANES
SCORE_BOUND_SLACK = 1.02
MAX_UNSHIFTED_SCORE = 64.0


def _dot(a, b):
    return jnp.dot(a, b, preferred_element_type=F32)


def _rms(x):
    return x * lax.rsqrt(jnp.mean(x * x, axis=-1, keepdims=True) + EPS)


def _adaln_kernel(cond_ref, w_ref, b_ref, o_ref):
    c = cond_ref[...]
    s = c / (1.0 + jnp.exp(-c))
    o_ref[...] = jnp.dot(s, w_ref[...], preferred_element_type=F32,
                         precision=lax.Precision.HIGHEST) + b_ref[...]


def _adaln(cond, w, b):
    rows, d = cond.shape
    n = w.shape[1]
    tn = n // 4
    return pl.pallas_call(
        _adaln_kernel,
        grid=(n // tn,),
        in_specs=[pl.BlockSpec((rows, d), lambda j: (0, 0)),
                  pl.BlockSpec((d, tn), lambda j: (0, j)),
                  pl.BlockSpec((1, tn), lambda j: (0, j))],
        out_specs=pl.BlockSpec((rows, tn), lambda j: (0, j)),
        out_shape=jax.ShapeDtypeStruct((rows, n), F32),
        compiler_params=pltpu.CompilerParams(dimension_semantics=("arbitrary",),
                                             vmem_limit_bytes=VMEM_LIMIT),
        name="adaln",
    )(cond, w, b)


def _proj_kernel(x_ref, sh_ref, sc_ref, g_ref, w_ref, qg_ref, kg_ref, bd_ref, cos_ref, sin_ref,
                 u_ref, qt_ref, k_ref, vt_ref):
    tm = x_ref.shape[0]
    h = _rms(x_ref[...]) * g_ref[...]
    h = h * (1.0 + sc_ref[...]) + sh_ref[...]
    p = _dot(h.astype(BF16), w_ref[...])
    o1 = FOURIER_WIDTH
    o2 = o1 + ATTN_WIDTH
    o3 = o2 + KV_WIDTH
    u_ref[...] = p[:, :o1].astype(u_ref.dtype)
    vt_ref[...] = p[:, o3:].T.astype(vt_ref.dtype)

    cos = cos_ref[...]
    sin = sin_ref[...]
    lane = lax.broadcasted_iota(jnp.int32, (tm, LANES), 1)
    first_half = (lane & 31) < 16

    def norm_rope(t, gain, bd):
        t2 = t * t
        hi = t2.astype(BF16)
        lo = (t2 - hi.astype(F32)).astype(BF16)
        ss = _dot(hi, bd) + _dot(lo, bd)
        tn = t * lax.rsqrt(ss * (1.0 / HEAD_DIM) + EPS) * gain
        outs = []
        for j in range(t.shape[1] // LANES):
            tj = tn[:, j * LANES:(j + 1) * LANES]
            partner = jnp.where(first_half,
                                pltpu.roll(tj, LANES - 16, axis=1),
                                pltpu.roll(tj, 16, axis=1))
            outs.append(tj * cos + partner * sin)
        return outs

    q_slabs = norm_rope(p[:, o1:o2], qg_ref[...], bd_ref[...])
    for j, qs in enumerate(q_slabs):
        qt_ref[j * LANES:(j + 1) * LANES, :] = (qs * Q_SCALE).T.astype(qt_ref.dtype)
    (k_rot,) = norm_rope(p[:, o2:o3], kg_ref[...], bd_ref[:KV_WIDTH, :KV_WIDTH])
    k_ref[...] = k_rot.astype(k_ref.dtype)


def _project(x, shift, scale, gain, w_in, q_gain, k_gain, bd, cos, sin, tm):
    b, s, d = x.shape
    in_w = w_in.shape[1]
    vec = pl.BlockSpec((None, 1, d), lambda bi, i: (bi, 0, 0))
    const = lambda shape: pl.BlockSpec(shape, lambda bi, i: (0,) * len(shape))
    return pl.pallas_call(
        _proj_kernel,
        grid=(b, s // tm),
        in_specs=[pl.BlockSpec((None, tm, d), lambda bi, i: (bi, i, 0)),
                  vec, vec, const((1, d)), const((d, in_w)),
                  const((1, ATTN_WIDTH)), const((1, KV_WIDTH)), const((ATTN_WIDTH, ATTN_WIDTH)),
                  pl.BlockSpec((tm, LANES), lambda bi, i: (i, 0)),
                  pl.BlockSpec((tm, LANES), lambda bi, i: (i, 0))],
        out_specs=[pl.BlockSpec((None, tm, FOURIER_WIDTH), lambda bi, i: (bi, i, 0)),
                   pl.BlockSpec((None, ATTN_WIDTH, tm), lambda bi, i: (bi, 0, i)),
                   pl.BlockSpec((None, tm, KV_WIDTH), lambda bi, i: (bi, i, 0)),
                   pl.BlockSpec((None, KV_WIDTH, tm), lambda bi, i: (bi, 0, i))],
        out_shape=[jax.ShapeDtypeStruct((b, s, FOURIER_WIDTH), BF16),
                   jax.ShapeDtypeStruct((b, ATTN_WIDTH, s), BF16),
                   jax.ShapeDtypeStruct((b, s, KV_WIDTH), BF16),
                   jax.ShapeDtypeStruct((b, KV_WIDTH, s), BF16)],
        compiler_params=pltpu.CompilerParams(dimension_semantics=("parallel", "parallel"),
                                             vmem_limit_bytes=VMEM_LIMIT),
        name="proj",
    )(x, shift, scale, gain, w_in, q_gain, k_gain, bd, cos, sin)


def _fourier_kernel(u_ref, ma_ref, mb_ref, mc_ref, wf_ref, o_ref, zr_sc, zi_sc, tsc, *, unroll):
    n2_count = DFT_INNER
    n1_count = u_ref.shape[0] // n2_count
    c = FOURIER_GROUP_DIM
    t_rows = 2 * n1_count
    t_pitch = t_rows + SUBLANES

    fw = _dot(mc_ref[...], wf_ref[...]).astype(BF16)
    fw_cat = jnp.concatenate([fw[:c], fw[c:]], axis=1)

    def fill(i, carry):
        src = pl.multiple_of(i * n2_count, n2_count)
        dst = pl.multiple_of(i * ROW_PITCH, SUBLANES)
        z = _dot(u_ref[pl.ds(src, n2_count), :], fw_cat)
        zr_sc[pl.ds(dst, n2_count), :] = z[:, :c]
        zi_sc[pl.ds(dst, n2_count), :] = z[:, c:]
        return carry
    lax.fori_loop(0, n1_count, fill, 0, unroll=unroll)

    def stage_a(n2, carry):
        zr = zr_sc[pl.ds(n2, n1_count, stride=ROW_PITCH), :]
        zi = zi_sc[pl.ds(n2, n1_count, stride=ROW_PITCH), :]
        z2 = jnp.concatenate([zr, zi], axis=0).astype(BF16)
        dst = pl.multiple_of(n2 * t_pitch, SUBLANES)
        tsc[pl.ds(dst, t_rows), :] = _dot(ma_ref[n2], z2)
        return carry
    lax.fori_loop(0, n2_count, stage_a, 0, unroll=unroll)

    def stage_b(k1, carry):
        tr = tsc[pl.ds(k1, n2_count, stride=t_pitch), :]
        ti = tsc[pl.ds(n1_count + k1, n2_count, stride=t_pitch), :]
        t2 = jnp.concatenate([tr, ti], axis=0).astype(BF16)
        o_ref[pl.ds(k1, n2_count, stride=n1_count), :] = _dot(mb_ref[...], t2)
        return carry
    lax.fori_loop(0, n1_count, stage_b, 0, unroll=unroll)


def _fourier(u, ma, mb, mc, w_four, unroll):
    b, s, _ = u.shape
    c = FOURIER_GROUP_DIM
    n1 = s // DFT_INNER
    single = pl.Buffered(1)
    return pl.pallas_call(
        functools.partial(_fourier_kernel, unroll=unroll),
        grid=(b, N_FOURIER_GROUPS),
        in_specs=[pl.BlockSpec((None, s, c), lambda bi, g: (bi, 0, g)),
                  pl.BlockSpec(ma.shape, lambda bi, g: (0, 0, 0), pipeline_mode=single),
                  pl.BlockSpec(mb.shape, lambda bi, g: (0, 0), pipeline_mode=single),
                  pl.BlockSpec(mc.shape, lambda bi, g: (0, 0), pipeline_mode=single),
                  pl.BlockSpec((None, c, c), lambda bi, g: (g, 0, 0))],
        out_specs=pl.BlockSpec((None, s, c), lambda bi, g: (bi, 0, g)),
        out_shape=jax.ShapeDtypeStruct((b, s, FOURIER_WIDTH), F32),
        scratch_shapes=[pltpu.VMEM((n1 * ROW_PITCH, c), F32),
                        pltpu.VMEM((n1 * ROW_PITCH, c), F32),
                        pltpu.VMEM((DFT_INNER * (2 * n1 + SUBLANES), c), F32)],
        compiler_params=pltpu.CompilerParams(dimension_semantics=("parallel", "parallel"),
                                             vmem_limit_bytes=VMEM_LIMIT),
        name="fourier",
    )(u, ma, mb, mc, w_four)


def _fourier_tables(s):
    n2c = DFT_INNER
    n1c = s // n2c
    k1 = np.arange(n1c, dtype=np.float64)
    n = (n2c * np.arange(n1c)[None, :] + np.arange(n2c)[:, None]).astype(np.float64)
    ang = 2.0 * np.pi * k1[None, :, None] * n[:, None, :] / s
    ca, sa = np.cos(ang), np.sin(ang)
    ma = np.concatenate([np.concatenate([ca, sa], axis=2),
                         np.concatenate([-sa, ca], axis=2)], axis=1)
    kk = np.arange(n2c, dtype=np.float64)
    a2 = 2.0 * np.pi * np.outer(kk, kk) / n2c
    mb = np.concatenate([np.cos(a2), np.sin(a2)], axis=1)
    cc = np.arange(FOURIER_GROUP_DIM, dtype=np.float64)
    ac = 2.0 * np.pi * np.outer(cc, cc) / FOURIER_GROUP_DIM
    norm = 1.0 / math.sqrt(s * FOURIER_GROUP_DIM)
    mc = np.concatenate([np.cos(ac), -np.sin(ac)], axis=0) * norm
    return (ma.astype(np.float32), mb.astype(np.float32), mc.astype(np.float32))


def _attn_kernel(qt_ref, k_ref, vt_ref, qg_ref, kg_ref, o_ref, acc_ref, *, tk, unroll):
    tq = qt_ref.shape[1]
    n_keys = k_ref.shape[0]
    g = pl.program_id(1)
    parts = []
    for h in range(Q_PER_KV):
        qh = qt_ref[h * HEAD_DIM:(h + 1) * HEAD_DIM, :]
        z = jnp.zeros_like(qh)
        parts.append(jnp.where(g == 0, jnp.concatenate([qh, z], axis=0),
                               jnp.concatenate([z, qh], axis=0)))
    w = jnp.concatenate(parts, axis=1)
    rq = w.shape[1]
    acc_ref[...] = jnp.zeros(acc_ref.shape, F32)

    bound = (jnp.max(jnp.abs(qg_ref[...])) * jnp.max(jnp.abs(kg_ref[...]))
             * (HEAD_DIM * Q_SCALE * SCORE_BOUND_SLACK))
    unshifted_ok = bound <= MAX_UNSHIFTED_SCORE

    def blocks(j):
        start = pl.multiple_of(j * tk, tk)
        return k_ref[pl.ds(start, tk), :], vt_ref[:, pl.ds(start, tk)]

    @pl.when(unshifted_ok)
    def _():
        def body(j, carry):
            kb, vb = blocks(j)
            p = jnp.exp2(_dot(kb, w)).astype(BF16)
            acc_ref[...] += _dot(vb, p)
            return carry
        lax.fori_loop(0, n_keys // tk, body, 0, unroll=unroll)

    @pl.when(jnp.logical_not(unshifted_ok))
    def _():
        def body(j, m_prev):
            kb, vb = blocks(j)
            s = _dot(kb, w)
            m_next = jnp.maximum(m_prev, jnp.max(s, axis=0, keepdims=True))
            alpha = jnp.exp2(m_prev - m_next)
            p = jnp.exp2(s - m_next).astype(BF16)
            acc_ref[...] = alpha * acc_ref[...] + _dot(vb, p)
            return m_next
        lax.fori_loop(0, n_keys // tk, body, jnp.full((1, rq), -jnp.inf, F32))

    acc = acc_ref[...]
    out_t = acc[:HEAD_DIM] / acc[HEAD_DIM:HEAD_DIM + 1]
    out_t = jnp.concatenate([out_t[:, h * tq:(h + 1) * tq] for h in range(Q_PER_KV)], axis=0)
    o_ref[...] = out_t.T.astype(o_ref.dtype)


def _attention(qt, k, vt_aug, q_gain, k_gain, tq, tk, unroll):
    b, _, s = qt.shape
    n_keys = k.shape[1]
    gw = Q_PER_KV * HEAD_DIM
    gain = pl.BlockSpec((1, HEAD_DIM), lambda bi, g, i: (0, 0))
    return pl.pallas_call(
        functools.partial(_attn_kernel, tk=tk, unroll=unroll),
        grid=(b, N_KV_HEADS, s // tq),
        in_specs=[pl.BlockSpec((None, gw, tq), lambda bi, g, i: (bi, g, i)),
                  pl.BlockSpec((None, n_keys, KV_WIDTH), lambda bi, g, i: (bi, 0, 0)),
                  pl.BlockSpec((None, VT_ROWS, n_keys), lambda bi, g, i: (bi, g, 0)),
                  gain, gain],
        out_specs=pl.BlockSpec((None, tq, gw), lambda bi, g, i: (bi, i, g)),
        out_shape=jax.ShapeDtypeStruct((b, s, ATTN_WIDTH), BF16),
        scratch_shapes=[pltpu.VMEM((VT_ROWS, Q_PER_KV * tq), F32)],
        compiler_params=pltpu.CompilerParams(
            dimension_semantics=("parallel", "parallel", "parallel"),
            vmem_limit_bytes=VMEM_LIMIT),
        name="attention",
    )(qt, k, vt_aug, q_gain, k_gain)


def _ffn_kernel(x_ref, fm_ref, at_ref, gt1_ref, sh2_ref, sc2_ref, gt2_ref, gffn_ref, gfin_ref,
                wo_ref, wg_ref, wu_ref, wd_ref, o_ref, *, ff_chunk):
    mix = (_dot(fm_ref[...].astype(BF16), wo_ref[:FOURIER_WIDTH, :])
           + _dot(at_ref[...], wo_ref[FOURIER_WIDTH:, :]))
    x1 = x_ref[...] + gt1_ref[...] * mix
    h2 = (_rms(x1) * gffn_ref[...] * (1.0 + sc2_ref[...]) + sh2_ref[...]).astype(BF16)
    d_ff = wg_ref.shape[1]
    ffn = None
    for c0 in range(0, d_ff, ff_chunk):
        gate = _dot(h2, wg_ref[:, c0:c0 + ff_chunk])
        up = _dot(h2, wu_ref[:, c0:c0 + ff_chunk])
        act = (gate / (1.0 + jnp.exp(-gate)) * up).astype(BF16)
        part = _dot(act, wd_ref[c0:c0 + ff_chunk, :])
        ffn = part if ffn is None else ffn + part
    x2 = x1 + gt2_ref[...] * ffn
    o_ref[...] = _rms(x2) * gfin_ref[...]


def _ffn(x, fm, at, gt1, sh2, sc2, gt2, g_ffn, g_final, w_out, w_gate, w_up, w_down, tm, ff_chunk):
    b, s, d = x.shape
    d_ff = w_gate.shape[1]
    vec = pl.BlockSpec((None, 1, d), lambda bi, i: (bi, 0, 0))
    single = pl.Buffered(1)
    const = lambda shape: pl.BlockSpec(shape, lambda bi, i: (0,) * len(shape))
    weight = lambda shape: pl.BlockSpec(shape, lambda bi, i: (0,) * len(shape), pipeline_mode=single)
    return pl.pallas_call(
        functools.partial(_ffn_kernel, ff_chunk=ff_chunk),
        grid=(b, s // tm),
        in_specs=[pl.BlockSpec((None, tm, d), lambda bi, i: (bi, i, 0)),
                  pl.BlockSpec((None, tm, FOURIER_WIDTH), lambda bi, i: (bi, i, 0)),
                  pl.BlockSpec((None, tm, ATTN_WIDTH), lambda bi, i: (bi, i, 0)),
                  vec, vec, vec, vec, const((1, d)), const((1, d)),
                  weight((d, d)), weight((d, d_ff)), weight((d, d_ff)), weight((d_ff, d))],
        out_specs=pl.BlockSpec((None, tm, d), lambda bi, i: (bi, i, 0)),
        out_shape=jax.ShapeDtypeStruct((b, s, d), F32),
        compiler_params=pltpu.CompilerParams(dimension_semantics=("parallel", "parallel"),
                                             vmem_limit_bytes=VMEM_LIMIT),
        name="ffn",
    )(x, fm, at, gt1, sh2, sc2, gt2, g_ffn, g_final, w_out, w_gate, w_up, w_down)


def _rope_tables(n_tokens):
    n_freq = HEAD_DIM // 4
    inv_freq = ROPE_THETA ** (-np.arange(n_freq, dtype=np.float64) / n_freq)
    t = np.arange(n_tokens)
    row_ang = (t // GRID_W).astype(np.float64)[:, None] * inv_freq[None, :]
    col_ang = (t % GRID_W).astype(np.float64)[:, None] * inv_freq[None, :]
    ang = np.concatenate([row_ang, row_ang, col_ang, col_ang], axis=1)
    sign = np.tile(np.concatenate([-np.ones(n_freq), np.ones(n_freq)]), 2)
    cos = np.cos(ang)
    sin = np.sin(ang) * sign[None, :]
    reps = LANES // HEAD_DIM
    return (np.tile(cos, (1, reps)).astype(np.float32), np.tile(sin, (1, reps)).astype(np.float32))


def _head_block_diag(width):
    idx = np.arange(width) // HEAD_DIM
    return (idx[:, None] == idx[None, :]).astype(np.float32)


def _key_block(n_keys):
    for tk in range(768, 0, -LANES):
        if n_keys % tk == 0:
            return tk
    raise ValueError(f"n_keys={n_keys} is not a multiple of {LANES}")


def kernel(x, c, ctx, c_ctx, w_ada, b_ada, g_mix, w_in, w_four, q_gain, k_gain, w_out, g_ffn,
           w_gate, w_up, w_down, g_final):
    b, s, d = x.shape
    n_ctx = ctx.shape[1]
    assert w_ada.shape[0] == 1, "single-layer block"
    assert s % (DFT_INNER * SUBLANES) == 0 and n_ctx % LANES == 0

    pad = (-(b + 1)) % SUBLANES
    cond = jnp.concatenate([c, c_ctx[None, :], jnp.zeros((pad, d), F32)], axis=0)
    ada = _adaln(cond, w_ada[0], b_ada[0][None, :])
    sh1, sc1, gt1, sh2, sc2, gt2 = [ada[:, i * d:(i + 1) * d] for i in range(6)]
    per_batch = lambda m: m[:b, None, :]
    for_ctx = lambda m: jnp.broadcast_to(m[b][None, None, :], (b, 1, d))

    w_in_b = w_in[0].astype(BF16)
    bd = jnp.asarray(_head_block_diag(ATTN_WIDTH)).astype(BF16)
    qg = jnp.tile(q_gain[0], N_HEADS)[None, :]
    kg = jnp.tile(k_gain[0], N_KV_HEADS)[None, :]
    g_mix2 = g_mix[0][None, :]

    cos, sin = _rope_tables(s)
    tm_proj = min(512, s)
    u, qt, k, vt = _project(x, per_batch(sh1), per_batch(sc1), g_mix2, w_in_b, qg, kg, bd,
                            jnp.asarray(cos), jnp.asarray(sin), tm_proj)
    _, _, kc, vtc = _project(ctx, for_ctx(sh1), for_ctx(sc1), g_mix2, w_in_b, qg, kg, bd,
                             jnp.ones((n_ctx, LANES), F32), jnp.zeros((n_ctx, LANES), F32), n_ctx)
    n_keys = s + n_ctx
    k_all = jnp.concatenate([k, kc], axis=1)
    vt_all = jnp.concatenate([vt, vtc], axis=2).reshape(b, N_KV_HEADS, HEAD_DIM, n_keys)
    ones_rows = jnp.zeros((b, N_KV_HEADS, VT_ROWS - HEAD_DIM, n_keys), BF16).at[:, :, 0, :].set(1.0)
    vt_aug = jnp.concatenate([vt_all, ones_rows], axis=2).reshape(b, N_KV_HEADS * VT_ROWS, n_keys)

    ma, mb, mc = _fourier_tables(s)
    fm = _fourier(u, jnp.asarray(ma).astype(BF16), jnp.asarray(mb).astype(BF16),
                  jnp.asarray(mc), w_four[0], unroll=4)

    at = _attention(qt, k_all, vt_aug, q_gain, k_gain, tq=min(256, s), tk=_key_block(n_keys), unroll=1)

    tm_ffn = min(512, s)
    d_ff = w_gate.shape[2]
    ff_chunk = d_ff // 2 if (d_ff // 2) % LANES == 0 else d_ff
    return _ffn(x, fm, at, per_batch(gt1), per_batch(sh2), per_batch(sc2), per_batch(gt2),
                g_ffn[0][None, :], g_final[None, :], w_out[0].astype(BF16), w_gate[0].astype(BF16),
                w_up[0].astype(BF16), w_down[0].astype(BF16), tm_ffn, ff_chunk)
```

```python
import functools
import math

import numpy as np
import jax
import jax.numpy as jnp
from jax import lax
from jax.experimental import pallas as pl
from jax.experimental.pallas import tpu as pltpu

F32 = jnp.float32
BF16 = jnp.bfloat16

EPS = 1e-6
ROPE_THETA = 10000.0
GRID_W = 64
HEAD_DIM = 64
N_HEADS = 8
N_KV_HEADS = 2
Q_PER_KV = N_HEADS // N_KV_HEADS
N_FOURIER_GROUPS = 4
FOURIER_GROUP_DIM = 128
FOURIER_WIDTH = N_FOURIER_GROUPS * FOURIER_GROUP_DIM
ATTN_WIDTH = N_HEADS * HEAD_DIM
KV_WIDTH = N_KV_HEADS * HEAD_DIM

LANES = 128
SUBLANES = 8
DFT_INNER = 128
ROW_PITCH = DFT_INNER + SUBLANES
VMEM_LIMIT = 56 * 1024 * 1024

Q_SCALE = (HEAD_DIM ** -0.5) * math.log2(math.e)
BF16_SUBLANES = 16
VT_ROWS = HEAD_DIM + BF16_SUBLANES
SCORE_BOUND_SLACK = 1.02
MAX_UNSHIFTED_SCORE = 64.0


def _dot(a, b):
    return jnp.dot(a, b, preferred_element_type=F32)


def _rms(x):
    return x * lax.rsqrt(jnp.mean(x * x, axis=-1, keepdims=True) + EPS)


def _adaln_kernel(cond_ref, w_ref, b_ref, o_ref):
    c = cond_ref[...]
    s = c / (1.0 + jnp.exp(-c))
    o_ref[...] = jnp.dot(s, w_ref[...], preferred_element_type=F32,
                         precision=lax.Precision.HIGHEST) + b_ref[...]


def _adaln(cond, w, b):
    rows, d = cond.shape
    n = w.shape[1]
    tn = n // 4
    return pl.pallas_call(
        _adaln_kernel,
        grid=(n // tn,),
        in_specs=[pl.BlockSpec((rows, d), lambda j: (0, 0)),
                  pl.BlockSpec((d, tn), lambda j: (0, j)),
                  pl.BlockSpec((1, tn), lambda j: (0, j))],
        out_specs=pl.BlockSpec((rows, tn), lambda j: (0, j)),
        out_shape=jax.ShapeDtypeStruct((rows, n), F32),
        compiler_params=pltpu.CompilerParams(dimension_semantics=("arbitrary",),
                                             vmem_limit_bytes=VMEM_LIMIT),
        name="adaln",
    )(cond, w, b)


def _proj_kernel(x_ref, sh_ref, sc_ref, g_ref, w_ref, qg_ref, kg_ref, bd_ref, cos_ref, sin_ref,
                 u_ref, qt_ref, k_ref, vt_ref):
    tm = x_ref.shape[0]
    h = _rms(x_ref[...]) * g_ref[...]
    h = h * (1.0 + sc_ref[...]) + sh_ref[...]
    p = _dot(h.astype(BF16), w_ref[...])
    o1 = FOURIER_WIDTH
    o2 = o1 + ATTN_WIDTH
    o3 = o2 + KV_WIDTH
    u_ref[...] = p[:, :o1].astype(u_ref.dtype)
    vt_ref[...] = p[:, o3:].T.astype(vt_ref.dtype)

    cos = cos_ref[...]
    sin = sin_ref[...]
    lane = lax.broadcasted_iota(jnp.int32, (tm, LANES), 1)
    first_half = (lane & 31) < 16

    def norm_rope(t, gain, bd):
        t2 = t * t
        hi = t2.astype(BF16)
        lo = (t2 - hi.astype(F32)).astype(BF16)
        ss = _dot(hi, bd) + _dot(lo, bd)
        tn = t * lax.rsqrt(ss * (1.0 / HEAD_DIM) + EPS) * gain
        outs = []
        for j in range(t.shape[1] // LANES):
            tj = tn[:, j * LANES:(j + 1) * LANES]
            partner = jnp.where(first_half,
                                pltpu.roll(tj, LANES - 16, axis=1),
                                pltpu.roll(tj, 16, axis=1))
            outs.append(tj * cos + partner * sin)
        return outs

    q_slabs = norm_rope(p[:, o1:o2], qg_ref[...], bd_ref[...])
    for j, qs in enumerate(q_slabs):
        qt_ref[j * LANES:(j + 1) * LANES, :] = (qs * Q_SCALE).T.astype(qt_ref.dtype)
    (k_rot,) = norm_rope(p[:, o2:o3], kg_ref[...], bd_ref[:KV_WIDTH, :KV_WIDTH])
    k_ref[...] = k_rot.astype(k_ref.dtype)


def _project(x, shift, scale, gain, w_in, q_gain, k_gain, bd, cos, sin, tm):
    b, s, d = x.shape
    in_w = w_in.shape[1]
    vec = pl.BlockSpec((None, 1, d), lambda bi, i: (bi, 0, 0))
    const = lambda shape: pl.BlockSpec(shape, lambda bi, i: (0,) * len(shape))
    return pl.pallas_call(
        _proj_kernel,
        grid=(b, s // tm),
        in_specs=[pl.BlockSpec((None, tm, d), lambda bi, i: (bi, i, 0)),
                  vec, vec, const((1, d)), const((d, in_w)),
                  const((1, ATTN_WIDTH)), const((1, KV_WIDTH)), const((ATTN_WIDTH, ATTN_WIDTH)),
                  pl.BlockSpec((tm, LANES), lambda bi, i: (i, 0)),
                  pl.BlockSpec((tm, LANES), lambda bi, i: (i, 0))],
        out_specs=[pl.BlockSpec((None, tm, FOURIER_WIDTH), lambda bi, i: (bi, i, 0)),
                   pl.BlockSpec((None, ATTN_WIDTH, tm), lambda bi, i: (bi, 0, i)),
                   pl.BlockSpec((None, tm, KV_WIDTH), lambda bi, i: (bi, i, 0)),
                   pl.BlockSpec((None, KV_WIDTH, tm), lambda bi, i: (bi, 0, i))],
        out_shape=[jax.ShapeDtypeStruct((b, s, FOURIER_WIDTH), BF16),
                   jax.ShapeDtypeStruct((b, ATTN_WIDTH, s), BF16),
                   jax.ShapeDtypeStruct((b, s, KV_WIDTH), BF16),
                   jax.ShapeDtypeStruct((b, KV_WIDTH, s), BF16)],
        compiler_params=pltpu.CompilerParams(dimension_semantics=("parallel", "parallel"),
                                             vmem_limit_bytes=VMEM_LIMIT),
        name="proj",
    )(x, shift, scale, gain, w_in, q_gain, k_gain, bd, cos, sin)


def _fourier_kernel(u_ref, ma_ref, mb_ref, mc_ref, wf_ref, o_ref, zr_sc, zi_sc, tsc, *, unroll):
    n2_count = DFT_INNER
    n1_count = u_ref.shape[0] // n2_count
    c = FOURIER_GROUP_DIM
    t_rows = 2 * n1_count
    t_pitch = t_rows + SUBLANES

    fw = _dot(mc_ref[...], wf_ref[...]).astype(BF16)
    fw_cat = jnp.concatenate([fw[:c], fw[c:]], axis=1)

    def fill(i, carry):
        src = pl.multiple_of(i * n2_count, n2_count)
        dst = pl.multiple_of(i * ROW_PITCH, SUBLANES)
        z = _dot(u_ref[pl.ds(src, n2_count), :], fw_cat)
        zr_sc[pl.ds(dst, n2_count), :] = z[:, :c]
        zi_sc[pl.ds(dst, n2_count), :] = z[:, c:]
        return carry
    lax.fori_loop(0, n1_count, fill, 0, unroll=unroll)

    def stage_a(n2, carry):
        zr = zr_sc[pl.ds(n2, n1_count, stride=ROW_PITCH), :]
        zi = zi_sc[pl.ds(n2, n1_count, stride=ROW_PITCH), :]
        z2 = jnp.concatenate([zr, zi], axis=0).astype(BF16)
        dst = pl.multiple_of(n2 * t_pitch, SUBLANES)
        tsc[pl.ds(dst, t_rows), :] = _dot(ma_ref[n2], z2)
        return carry
    lax.fori_loop(0, n2_count, stage_a, 0, unroll=unroll)

    def stage_b(k1, carry):
        tr = tsc[pl.ds(k1, n2_count, stride=t_pitch), :]
        ti = tsc[pl.ds(n1_count + k1, n2_count, stride=t_pitch), :]
        t2 = jnp.concatenate([tr, ti], axis=0).astype(BF16)
        o_ref[pl.ds(k1, n2_count, stride=n1_count), :] = _dot(mb_ref[...], t2)
        return carry
    lax.fori_loop(0, n1_count, stage_b, 0, unroll=unroll)


def _fourier(u, ma, mb, mc, w_four, unroll):
    b, s, _ = u.shape
    c = FOURIER_GROUP_DIM
    n1 = s // DFT_INNER
    single = pl.Buffered(1)
    return pl.pallas_call(
        functools.partial(_fourier_kernel, unroll=unroll),
        grid=(b, N_FOURIER_GROUPS),
        in_specs=[pl.BlockSpec((None, s, c), lambda bi, g: (bi, 0, g)),
                  pl.BlockSpec(ma.shape, lambda bi, g: (0, 0, 0), pipeline_mode=single),
                  pl.BlockSpec(mb.shape, lambda bi, g: (0, 0), pipeline_mode=single),
                  pl.BlockSpec(mc.shape, lambda bi, g: (0, 0), pipeline_mode=single),
                  pl.BlockSpec((None, c, c), lambda bi, g: (g, 0, 0))],
        out_specs=pl.BlockSpec((None, s, c), lambda bi, g: (bi, 0, g)),
        out_shape=jax.ShapeDtypeStruct((b, s, FOURIER_WIDTH), F32),
        scratch_shapes=[pltpu.VMEM((n1 * ROW_PITCH, c), F32),
                        pltpu.VMEM((n1 * ROW_PITCH, c), F32),
                        pltpu.VMEM((DFT_INNER * (2 * n1 + SUBLANES), c), F32)],
        compiler_params=pltpu.CompilerParams(dimension_semantics=("parallel", "parallel"),
                                             vmem_limit_bytes=VMEM_LIMIT),
        name="fourier",
    )(u, ma, mb, mc, w_four)


def _fourier_tables(s):
    n2c = DFT_INNER
    n1c = s // n2c
    k1 = np.arange(n1c, dtype=np.float64)
    n = (n2c * np.arange(n1c)[None, :] + np.arange(n2c)[:, None]).astype(np.float64)
    ang = 2.0 * np.pi * k1[None, :, None] * n[:, None, :] / s
    ca, sa = np.cos(ang), np.sin(ang)
    ma = np.concatenate([np.concatenate([ca, sa], axis=2),
                         np.concatenate([-sa, ca], axis=2)], axis=1)
    kk = np.arange(n2c, dtype=np.float64)
    a2 = 2.0 * np.pi * np.outer(kk, kk) / n2c
    mb = np.concatenate([np.cos(a2), np.sin(a2)], axis=1)
    cc = np.arange(FOURIER_GROUP_DIM, dtype=np.float64)
    ac = 2.0 * np.pi * np.outer(cc, cc) / FOURIER_GROUP_DIM
    norm = 1.0 / math.sqrt(s * FOURIER_GROUP_DIM)
    mc = np.concatenate([np.cos(ac), -np.sin(ac)], axis=0) * norm
    return (ma.astype(np.float32), mb.astype(np.float32), mc.astype(np.float32))


def _attn_kernel(qt_ref, k_ref, vt_ref, qg_ref, kg_ref, o_ref, acc_ref, p_ref, *, tk, unroll):
    tq = qt_ref.shape[1]
    n_keys = k_ref.shape[0]
    g = pl.program_id(1)
    parts = []
    for h in range(Q_PER_KV):
        qh = qt_ref[h * HEAD_DIM:(h + 1) * HEAD_DIM, :]
        z = jnp.zeros_like(qh)
        parts.append(jnp.where(g == 0, jnp.concatenate([qh, z], axis=0),
                               jnp.concatenate([z, qh], axis=0)))
    w = jnp.concatenate(parts, axis=1)
    rq = w.shape[1]
    acc_ref[...] = jnp.zeros(acc_ref.shape, F32)

    bound = (jnp.max(jnp.abs(qg_ref[...])) * jnp.max(jnp.abs(kg_ref[...]))
             * (HEAD_DIM * Q_SCALE * SCORE_BOUND_SLACK))
    unshifted_ok = bound <= MAX_UNSHIFTED_SCORE

    def blocks(j):
        start = pl.multiple_of(j * tk, tk)
        return k_ref[pl.ds(start, tk), :], vt_ref[:, pl.ds(start, tk)]

    @pl.when(unshifted_ok)
    def _():
        kb0, _ = blocks(0)
        p_ref[...] = jnp.exp2(_dot(kb0, w)).astype(BF16)

        def body(j, carry):
            kb, _ = blocks(j)
            _, vb_prev = blocks(j - 1)
            s = _dot(kb, w)
            acc_ref[...] += _dot(vb_prev, p_ref[...])
            p_ref[...] = jnp.exp2(s).astype(BF16)
            return carry
        n_blocks = n_keys // tk
        lax.fori_loop(1, n_blocks, body, 0, unroll=unroll)
        _, vb_last = blocks(n_blocks - 1)
        acc_ref[...] += _dot(vb_last, p_ref[...])

    @pl.when(jnp.logical_not(unshifted_ok))
    def _():
        def body(j, m_prev):
            kb, vb = blocks(j)
            s = _dot(kb, w)
            m_next = jnp.maximum(m_prev, jnp.max(s, axis=0, keepdims=True))
            alpha = jnp.exp2(m_prev - m_next)
            p = jnp.exp2(s - m_next).astype(BF16)
            acc_ref[...] = alpha * acc_ref[...] + _dot(vb, p)
            return m_next
        lax.fori_loop(0, n_keys // tk, body, jnp.full((1, rq), -jnp.inf, F32))

    acc = acc_ref[...]
    out_t = acc[:HEAD_DIM] / acc[HEAD_DIM:HEAD_DIM + 1]
    out_t = jnp.concatenate([out_t[:, h * tq:(h + 1) * tq] for h in range(Q_PER_KV)], axis=0)
    o_ref[...] = out_t.T.astype(o_ref.dtype)


def _attention(qt, k, vt_aug, q_gain, k_gain, tq, tk, unroll):
    b, _, s = qt.shape
    n_keys = k.shape[1]
    gw = Q_PER_KV * HEAD_DIM
    gain = pl.BlockSpec((1, HEAD_DIM), lambda bi, g, i: (0, 0))
    return pl.pallas_call(
        functools.partial(_attn_kernel, tk=tk, unroll=unroll),
        grid=(b, N_KV_HEADS, s // tq),
        in_specs=[pl.BlockSpec((None, gw, tq), lambda bi, g, i: (bi, g, i)),
                  pl.BlockSpec((None, n_keys, KV_WIDTH), lambda bi, g, i: (bi, 0, 0)),
                  pl.BlockSpec((None, VT_ROWS, n_keys), lambda bi, g, i: (bi, g, 0)),
                  gain, gain],
        out_specs=pl.BlockSpec((None, tq, gw), lambda bi, g, i: (bi, i, g)),
        out_shape=jax.ShapeDtypeStruct((b, s, ATTN_WIDTH), BF16),
        scratch_shapes=[pltpu.VMEM((VT_ROWS, Q_PER_KV * tq), F32),
                        pltpu.VMEM((tk, Q_PER_KV * tq), BF16)],
        compiler_params=pltpu.CompilerParams(
            dimension_semantics=("parallel", "parallel", "parallel"),
            vmem_limit_bytes=VMEM_LIMIT),
        name="attention",
    )(qt, k, vt_aug, q_gain, k_gain)


def _ffn_kernel(x_ref, fm_ref, at_ref, gt1_ref, sh2_ref, sc2_ref, gt2_ref, gffn_ref, gfin_ref,
                wo_ref, wg_ref, wu_ref, wd_ref, o_ref, *, ff_chunk):
    mix = (_dot(fm_ref[...].astype(BF16), wo_ref[:FOURIER_WIDTH, :])
           + _dot(at_ref[...], wo_ref[FOURIER_WIDTH:, :]))
    x1 = x_ref[...] + gt1_ref[...] * mix
    h2 = (_rms(x1) * gffn_ref[...] * (1.0 + sc2_ref[...]) + sh2_ref[...]).astype(BF16)
    d_ff = wg_ref.shape[1]
    ffn = None
    for c0 in range(0, d_ff, ff_chunk):
        gate = _dot(h2, wg_ref[:, c0:c0 + ff_chunk])
        up = _dot(h2, wu_ref[:, c0:c0 + ff_chunk])
        act = (gate / (1.0 + jnp.exp(-gate)) * up).astype(BF16)
        part = _dot(act, wd_ref[c0:c0 + ff_chunk, :])
        ffn = part if ffn is None else ffn + part
    x2 = x1 + gt2_ref[...] * ffn
    o_ref[...] = _rms(x2) * gfin_ref[...]


def _ffn(x, fm, at, gt1, sh2, sc2, gt2, g_ffn, g_final, w_out, w_gate, w_up, w_down, tm, ff_chunk):
    b, s, d = x.shape
    d_ff = w_gate.shape[1]
    vec = pl.BlockSpec((None, 1, d), lambda bi, i: (bi, 0, 0))
    single = pl.Buffered(1)
    const = lambda shape: pl.BlockSpec(shape, lambda bi, i: (0,) * len(shape))
    weight = lambda shape: pl.BlockSpec(shape, lambda bi, i: (0,) * len(shape), pipeline_mode=single)
    return pl.pallas_call(
        functools.partial(_ffn_kernel, ff_chunk=ff_chunk),
        grid=(b, s // tm),
        in_specs=[pl.BlockSpec((None, tm, d), lambda bi, i: (bi, i, 0)),
                  pl.BlockSpec((None, tm, FOURIER_WIDTH), lambda bi, i: (bi, i, 0)),
                  pl.BlockSpec((None, tm, ATTN_WIDTH), lambda bi, i: (bi, i, 0)),
                  vec, vec, vec, vec, const((1, d)), const((1, d)),
                  weight((d, d)), weight((d, d_ff)), weight((d, d_ff)), weight((d_ff, d))],
        out_specs=pl.BlockSpec((None, tm, d), lambda bi, i: (bi, i, 0)),
        out_shape=jax.ShapeDtypeStruct((b, s, d), F32),
        compiler_params=pltpu.CompilerParams(dimension_semantics=("parallel", "parallel"),
                                             vmem_limit_bytes=VMEM_LIMIT),
        name="ffn",
    )(x, fm, at, gt1, sh2, sc2, gt2, g_ffn, g_final, w_out, w_gate, w_up, w_down)


def _rope_tables(n_tokens):
    n_freq = HEAD_DIM // 4
    inv_freq = ROPE_THETA ** (-np.arange(n_freq, dtype=np.float64) / n_freq)
    t = np.arange(n_tokens)
    row_ang = (t // GRID_W).astype(np.float64)[:, None] * inv_freq[None, :]
    col_ang = (t % GRID_W).astype(np.float64)[:, None] * inv_freq[None, :]
    ang = np.concatenate([row_ang, row_ang, col_ang, col_ang], axis=1)
    sign = np.tile(np.concatenate([-np.ones(n_freq), np.ones(n_freq)]), 2)
    cos = np.cos(ang)
    sin = np.sin(ang) * sign[None, :]
    reps = LANES // HEAD_DIM
    return (np.tile(cos, (1, reps)).astype(np.float32), np.tile(sin, (1, reps)).astype(np.float32))


def _head_block_diag(width):
    idx = np.arange(width) // HEAD_DIM
    return (idx[:, None] == idx[None, :]).astype(np.float32)


def _key_block(n_keys):
    for tk in range(768, 0, -LANES):
        if n_keys % tk == 0:
            return tk
    raise ValueError(f"n_keys={n_keys} is not a multiple of {LANES}")


def kernel(x, c, ctx, c_ctx, w_ada, b_ada, g_mix, w_in, w_four, q_gain, k_gain, w_out, g_ffn,
           w_gate, w_up, w_down, g_final):
    b, s, d = x.shape
    n_ctx = ctx.shape[1]
    assert w_ada.shape[0] == 1, "single-layer block"
    assert s % (DFT_INNER * SUBLANES) == 0 and n_ctx % LANES == 0

    pad = (-(b + 1)) % SUBLANES
    cond = jnp.concatenate([c, c_ctx[None, :], jnp.zeros((pad, d), F32)], axis=0)
    ada = _adaln(cond, w_ada[0], b_ada[0][None, :])
    sh1, sc1, gt1, sh2, sc2, gt2 = [ada[:, i * d:(i + 1) * d] for i in range(6)]
    per_batch = lambda m: m[:b, None, :]
    for_ctx = lambda m: jnp.broadcast_to(m[b][None, None, :], (b, 1, d))

    w_in_b = w_in[0].astype(BF16)
    bd = jnp.asarray(_head_block_diag(ATTN_WIDTH)).astype(BF16)
    qg = jnp.tile(q_gain[0], N_HEADS)[None, :]
    kg = jnp.tile(k_gain[0], N_KV_HEADS)[None, :]
    g_mix2 = g_mix[0][None, :]

    cos, sin = _rope_tables(s)
    tm_proj = min(512, s)
    u, qt, k, vt = _project(x, per_batch(sh1), per_batch(sc1), g_mix2, w_in_b, qg, kg, bd,
                            jnp.asarray(cos), jnp.asarray(sin), tm_proj)
    _, _, kc, vtc = _project(ctx, for_ctx(sh1), for_ctx(sc1), g_mix2, w_in_b, qg, kg, bd,
                             jnp.ones((n_ctx, LANES), F32), jnp.zeros((n_ctx, LANES), F32), n_ctx)
    n_keys = s + n_ctx
    k_all = jnp.concatenate([k, kc], axis=1)
    vt_all = jnp.concatenate([vt, vtc], axis=2).reshape(b, N_KV_HEADS, HEAD_DIM, n_keys)
    ones_rows = jnp.zeros((b, N_KV_HEADS, VT_ROWS - HEAD_DIM, n_keys), BF16).at[:, :, 0, :].set(1.0)
    vt_aug = jnp.concatenate([vt_all, ones_rows], axis=2).reshape(b, N_KV_HEADS * VT_ROWS, n_keys)

    ma, mb, mc = _fourier_tables(s)
    fm = _fourier(u, jnp.asarray(ma).astype(BF16), jnp.asarray(mb).astype(BF16),
                  jnp.asarray(mc), w_four[0], unroll=4)

    at = _attention(qt, k_all, vt_aug, q_gain, k_gain, tq=min(256, s), tk=_key_block(n_keys), unroll=True)

    tm_ffn = min(512, s)
    d_ff = w_gate.shape[2]
    ff_chunk = d_ff // 2 if (d_ff // 2) % LANES == 0 else d_ff
    return _ffn(x, fm, at, per_batch(gt1), per_batch(sh2), per_batch(sc2), per_batch(gt2),
                g_ffn[0][None, :], g_final[None, :], w_out[0].astype(BF16), w_gate[0].astype(BF16),
                w_up[0].astype(BF16), w_down[0].astype(BF16), tm_ffn, ff_chunk)
```

```python
import functools
import math

import numpy as np
import jax
import jax.numpy as jnp
from jax import lax
from jax.experimental import pallas as pl
from jax.experimental.pallas import tpu as pltpu

F32 = jnp.float32
BF16 = jnp.bfloat16

EPS = 1e-6
ROPE_THETA = 10000.0
GRID_W = 64
HEAD_DIM = 64
N_HEADS = 8
N_KV_HEADS = 2
Q_PER_KV = N_HEADS // N_KV_HEADS
N_FOURIER_GROUPS = 4
FOURIER_GROUP_DIM = 128
FOURIER_WIDTH = N_FOURIER_GROUPS * FOURIER_GROUP_DIM
ATTN_WIDTH = N_HEADS * HEAD_DIM
KV_WIDTH = N_KV_HEADS * HEAD_DIM

LANES = 128
SUBLANES = 8
DFT_INNER = 128
ROW_PITCH = DFT_INNER + SUBLANES
VMEM_LIMIT = 56 * 1024 * 1024

Q_SCALE = (HEAD_DIM ** -0.5) * math.log2(math.e)
BF16_SUBLANES = 16
VT_ROWS = HEAD_DIM + BF16_SUBLANES
SCORE_BOUND_SLACK = 1.02
MAX_UNSHIFTED_SCORE = 64.0


def _dot(a, b):
    return jnp.dot(a, b, preferred_element_type=F32)


def _split_bf16(a):
    hi = a.astype(BF16)
    return hi, (a - hi.astype(F32)).astype(BF16)


def _rms(x):
    return x * lax.rsqrt(jnp.mean(x * x, axis=-1, keepdims=True) + EPS)


def _adaln_kernel(cond_ref, w_ref, b_ref, o_ref):
    c = cond_ref[...]
    s_hi, s_lo = _split_bf16(c / (1.0 + jnp.exp(-c)))
    w_hi, w_lo = _split_bf16(w_ref[...])
    o_ref[...] = _dot(s_hi, w_hi) + _dot(s_lo, w_hi) + _dot(s_hi, w_lo) + b_ref[...]


def _adaln(cond, w, b):
    rows, d = cond.shape
    n = w.shape[1]
    tn = n // 4
    return pl.pallas_call(
        _adaln_kernel,
        grid=(n // tn,),
        in_specs=[pl.BlockSpec((rows, d), lambda j: (0, 0)),
                  pl.BlockSpec((d, tn), lambda j: (0, j)),
                  pl.BlockSpec((1, tn), lambda j: (0, j))],
        out_specs=pl.BlockSpec((rows, tn), lambda j: (0, j)),
        out_shape=jax.ShapeDtypeStruct((rows, n), F32),
        compiler_params=pltpu.CompilerParams(dimension_semantics=("arbitrary",),
                                             vmem_limit_bytes=VMEM_LIMIT),
        name="adaln",
    )(cond, w, b)


def _proj_kernel(x_ref, sh_ref, sc_ref, g_ref, w_ref, qg_ref, kg_ref, bd_ref, cos_ref, sin_ref,
                 u_ref, qt_ref, k_ref, vt_ref):
    tm = x_ref.shape[0]
    h = _rms(x_ref[...]) * g_ref[...]
    h = h * (1.0 + sc_ref[...]) + sh_ref[...]
    p = _dot(h.astype(BF16), w_ref[...])
    o1 = FOURIER_WIDTH
    o2 = o1 + ATTN_WIDTH
    o3 = o2 + KV_WIDTH
    u_ref[...] = p[:, :o1].astype(u_ref.dtype)
    vt_ref[...] = p[:, o3:].T.astype(vt_ref.dtype)

    cos = cos_ref[...]
    sin = sin_ref[...]
    lane = lax.broadcasted_iota(jnp.int32, (tm, LANES), 1)
    first_half = (lane & 31) < 16

    def norm_rope(t, gain, bd):
        hi, lo = _split_bf16(t * t)
        ss = _dot(hi, bd) + _dot(lo, bd)
        tn = t * lax.rsqrt(ss * (1.0 / HEAD_DIM) + EPS) * gain
        outs = []
        for j in range(t.shape[1] // LANES):
            tj = tn[:, j * LANES:(j + 1) * LANES]
            partner = jnp.where(first_half,
                                pltpu.roll(tj, LANES - 16, axis=1),
                                pltpu.roll(tj, 16, axis=1))
            outs.append(tj * cos + partner * sin)
        return outs

    q_slabs = norm_rope(p[:, o1:o2], qg_ref[...], bd_ref[...])
    for j, qs in enumerate(q_slabs):
        qt_ref[j * LANES:(j + 1) * LANES, :] = (qs * Q_SCALE).T.astype(qt_ref.dtype)
    (k_rot,) = norm_rope(p[:, o2:o3], kg_ref[...], bd_ref[:KV_WIDTH, :KV_WIDTH])
    k_ref[...] = k_rot.astype(k_ref.dtype)


def _project(x, shift, scale, gain, w_in, q_gain, k_gain, bd, cos, sin, tm):
    b, s, d = x.shape
    in_w = w_in.shape[1]
    vec = pl.BlockSpec((None, 1, d), lambda bi, i: (bi, 0, 0))
    const = lambda shape: pl.BlockSpec(shape, lambda bi, i: (0,) * len(shape))
    return pl.pallas_call(
        _proj_kernel,
        grid=(b, s // tm),
        in_specs=[pl.BlockSpec((None, tm, d), lambda bi, i: (bi, i, 0)),
                  vec, vec, const((1, d)), const((d, in_w)),
                  const((1, ATTN_WIDTH)), const((1, KV_WIDTH)), const((ATTN_WIDTH, ATTN_WIDTH)),
                  pl.BlockSpec((tm, LANES), lambda bi, i: (i, 0)),
                  pl.BlockSpec((tm, LANES), lambda bi, i: (i, 0))],
        out_specs=[pl.BlockSpec((None, tm, FOURIER_WIDTH), lambda bi, i: (bi, i, 0)),
                   pl.BlockSpec((None, ATTN_WIDTH, tm), lambda bi, i: (bi, 0, i)),
                   pl.BlockSpec((None, tm, KV_WIDTH), lambda bi, i: (bi, i, 0)),
                   pl.BlockSpec((None, KV_WIDTH, tm), lambda bi, i: (bi, 0, i))],
        out_shape=[jax.ShapeDtypeStruct((b, s, FOURIER_WIDTH), BF16),
                   jax.ShapeDtypeStruct((b, ATTN_WIDTH, s), BF16),
                   jax.ShapeDtypeStruct((b, s, KV_WIDTH), BF16),
                   jax.ShapeDtypeStruct((b, KV_WIDTH, s), BF16)],
        compiler_params=pltpu.CompilerParams(dimension_semantics=("parallel", "parallel"),
                                             vmem_limit_bytes=VMEM_LIMIT),
        name="proj",
    )(x, shift, scale, gain, w_in, q_gain, k_gain, bd, cos, sin)


def _fourier_kernel(u_ref, ma_ref, mb_ref, mc_ref, wf_ref, o_ref, zr_sc, zi_sc, tsc, *, unroll):
    n2_count = DFT_INNER
    n1_count = u_ref.shape[0] // n2_count
    c = FOURIER_GROUP_DIM
    t_rows = 2 * n1_count
    t_pitch = t_rows + SUBLANES
    o_pitch = n1_count + SUBLANES

    fw = _dot(mc_ref[...], wf_ref[...]).astype(BF16)
    fw_cat = jnp.concatenate([fw[:c], fw[c:]], axis=1)

    def fill(i, carry):
        src = pl.multiple_of(i * n2_count, n2_count)
        dst = pl.multiple_of(i * ROW_PITCH, SUBLANES)
        z = _dot(u_ref[pl.ds(src, n2_count), :], fw_cat)
        zr_sc[pl.ds(dst, n2_count), :] = z[:, :c]
        zi_sc[pl.ds(dst, n2_count), :] = z[:, c:]
        return carry
    lax.fori_loop(0, n1_count, fill, 0, unroll=min(unroll, n1_count))

    def stage_a(n2, carry):
        zr = zr_sc[pl.ds(n2, n1_count, stride=ROW_PITCH), :]
        zi = zi_sc[pl.ds(n2, n1_count, stride=ROW_PITCH), :]
        z2 = jnp.concatenate([zr, zi], axis=0).astype(BF16)
        dst = pl.multiple_of(n2 * t_pitch, SUBLANES)
        tsc[pl.ds(dst, t_rows), :] = _dot(ma_ref[n2], z2)
        return carry
    lax.fori_loop(0, n2_count, stage_a, 0, unroll=min(2 * unroll, n2_count))

    def stage_b(k1, carry):
        tr = tsc[pl.ds(k1, n2_count, stride=t_pitch), :]
        ti = tsc[pl.ds(n1_count + k1, n2_count, stride=t_pitch), :]
        t2 = jnp.concatenate([tr, ti], axis=0).astype(BF16)
        o_ref[pl.ds(k1, n2_count, stride=o_pitch), :] = _dot(mb_ref[...], t2)
        return carry
    lax.fori_loop(0, n1_count, stage_b, 0, unroll=min(unroll, n1_count))
    for pad_row in range(SUBLANES):
        o_ref[pl.ds(n1_count + pad_row, n2_count, stride=o_pitch), :] = jnp.zeros((n2_count, c), F32)


def _fourier(u, ma, mb, mc, w_four, unroll):
    b, s, _ = u.shape
    c = FOURIER_GROUP_DIM
    n1 = s // DFT_INNER
    single = pl.Buffered(1)
    return pl.pallas_call(
        functools.partial(_fourier_kernel, unroll=unroll),
        grid=(b, N_FOURIER_GROUPS),
        in_specs=[pl.BlockSpec((None, s, c), lambda bi, g: (bi, 0, g)),
                  pl.BlockSpec(ma.shape, lambda bi, g: (0, 0, 0), pipeline_mode=single),
                  pl.BlockSpec(mb.shape, lambda bi, g: (0, 0), pipeline_mode=single),
                  pl.BlockSpec(mc.shape, lambda bi, g: (0, 0), pipeline_mode=single),
                  pl.BlockSpec((None, c, c), lambda bi, g: (g, 0, 0))],
        out_specs=pl.BlockSpec((None, DFT_INNER * (n1 + SUBLANES), c), lambda bi, g: (bi, 0, g)),
        out_shape=jax.ShapeDtypeStruct((b, DFT_INNER * (n1 + SUBLANES), FOURIER_WIDTH), F32),
        scratch_shapes=[pltpu.VMEM((n1 * ROW_PITCH, c), F32),
                        pltpu.VMEM((n1 * ROW_PITCH, c), F32),
                        pltpu.VMEM((DFT_INNER * (2 * n1 + SUBLANES), c), F32)],
        compiler_params=pltpu.CompilerParams(dimension_semantics=("parallel", "parallel"),
                                             vmem_limit_bytes=VMEM_LIMIT),
        name="fourier",
    )(u, ma, mb, mc, w_four)


def _fourier_tables(s):
    n2c = DFT_INNER
    n1c = s // n2c
    k1 = np.arange(n1c, dtype=np.float64)
    n = (n2c * np.arange(n1c)[None, :] + np.arange(n2c)[:, None]).astype(np.float64)
    ang = 2.0 * np.pi * k1[None, :, None] * n[:, None, :] / s
    ca, sa = np.cos(ang), np.sin(ang)
    ma = np.concatenate([np.concatenate([ca, sa], axis=2),
                         np.concatenate([-sa, ca], axis=2)], axis=1)
    kk = np.arange(n2c, dtype=np.float64)
    a2 = 2.0 * np.pi * np.outer(kk, kk) / n2c
    mb = np.concatenate([np.cos(a2), np.sin(a2)], axis=1)
    cc = np.arange(FOURIER_GROUP_DIM, dtype=np.float64)
    ac = 2.0 * np.pi * np.outer(cc, cc) / FOURIER_GROUP_DIM
    norm = 1.0 / math.sqrt(s * FOURIER_GROUP_DIM)
    mc = np.concatenate([np.cos(ac), -np.sin(ac)], axis=0) * norm
    return (ma.astype(np.float32), mb.astype(np.float32), mc.astype(np.float32))


def _attn_kernel(bound_ref, qt_ref, k_ref, vt_ref, o_ref, acc_ref, p_ref, *, tk, unroll):
    tq = qt_ref.shape[1]
    n_keys = k_ref.shape[0]
    g = pl.program_id(1)
    parts = []
    for h in range(Q_PER_KV):
        qh = qt_ref[h * HEAD_DIM:(h + 1) * HEAD_DIM, :]
        z = jnp.zeros_like(qh)
        parts.append(jnp.where(g == 0, jnp.concatenate([qh, z], axis=0),
                               jnp.concatenate([z, qh], axis=0)))
    w = jnp.concatenate(parts, axis=1)
    rq = w.shape[1]
    acc_ref[...] = jnp.zeros(acc_ref.shape, F32)

    unshifted_ok = bound_ref[0] <= MAX_UNSHIFTED_SCORE

    def blocks(j):
        start = pl.multiple_of(j * tk, tk)
        return k_ref[pl.ds(start, tk), :], vt_ref[:, pl.ds(start, tk)]

    @pl.when(unshifted_ok)
    def _():
        kb0, _ = blocks(0)
        p_ref[...] = jnp.exp2(_dot(kb0, w)).astype(BF16)

        def body(j, carry):
            kb, _ = blocks(j)
            _, vb_prev = blocks(j - 1)
            s = _dot(kb, w)
            acc_ref[...] += _dot(vb_prev, p_ref[...])
            p_ref[...] = jnp.exp2(s).astype(BF16)
            return carry
        n_blocks = n_keys // tk
        lax.fori_loop(1, n_blocks, body, 0, unroll=unroll)
        _, vb_last = blocks(n_blocks - 1)
        acc_ref[...] += _dot(vb_last, p_ref[...])

    @pl.when(jnp.logical_not(unshifted_ok))
    def _():
        def body(j, m_prev):
            kb, vb = blocks(j)
            s = _dot(kb, w)
            m_next = jnp.maximum(m_prev, jnp.max(s, axis=0, keepdims=True))
            alpha = jnp.exp2(m_prev - m_next)
            p = jnp.exp2(s - m_next).astype(BF16)
            acc_ref[...] = alpha * acc_ref[...] + _dot(vb, p)
            return m_next
        lax.fori_loop(0, n_keys // tk, body, jnp.full((1, rq), -jnp.inf, F32))

    acc = acc_ref[...]
    out_t = acc[:HEAD_DIM] / acc[HEAD_DIM:HEAD_DIM + 1]
    out_t = jnp.concatenate([out_t[:, h * tq:(h + 1) * tq] for h in range(Q_PER_KV)], axis=0)
    o_ref[...] = out_t.T.astype(o_ref.dtype)


def _attention(qt, k, vt_aug, q_gain, k_gain, tq, tk, unroll):
    b, _, s = qt.shape
    n_keys = k.shape[1]
    gw = Q_PER_KV * HEAD_DIM
    bound = (jnp.max(jnp.abs(q_gain)) * jnp.max(jnp.abs(k_gain))
             * (HEAD_DIM * Q_SCALE * SCORE_BOUND_SLACK)).reshape(1).astype(F32)
    return pl.pallas_call(
        functools.partial(_attn_kernel, tk=tk, unroll=unroll),
        grid=(b, N_KV_HEADS, s // tq),
        in_specs=[pl.BlockSpec(memory_space=pltpu.SMEM),
                  pl.BlockSpec((None, gw, tq), lambda bi, g, i: (bi, g, i)),
                  pl.BlockSpec((None, n_keys, KV_WIDTH), lambda bi, g, i: (bi, 0, 0)),
                  pl.BlockSpec((None, VT_ROWS, n_keys), lambda bi, g, i: (bi, g, 0))],
        out_specs=pl.BlockSpec((None, tq, gw), lambda bi, g, i: (bi, i, g)),
        out_shape=jax.ShapeDtypeStruct((b, s, ATTN_WIDTH), BF16),
        scratch_shapes=[pltpu.VMEM((VT_ROWS, Q_PER_KV * tq), F32),
                        pltpu.VMEM((tk, Q_PER_KV * tq), BF16)],
        compiler_params=pltpu.CompilerParams(
            dimension_semantics=("parallel", "parallel", "parallel"),
            vmem_limit_bytes=VMEM_LIMIT),
        name="attention",
    )(bound, qt, k, vt_aug)


def _ffn_kernel(x_ref, fm_ref, at_ref, gt1_ref, sh2_ref, sc2_ref, gt2_ref, gffn_ref, gfin_ref,
                wo_ref, wg_ref, wu_ref, wd_ref, o_ref, *, ff_chunk, row_groups):
    tm = x_ref.shape[0]
    gm = tm // row_groups
    rows = [slice(i * gm, (i + 1) * gm) for i in range(row_groups)]
    n1 = fm_ref.shape[1] - SUBLANES
    k2g = gm // n1
    fms = [fm_ref[i * k2g:(i + 1) * k2g, :n1, :].reshape(gm, FOURIER_WIDTH) for i in range(row_groups)]
    mix = [_dot(f.astype(BF16), wo_ref[:FOURIER_WIDTH, :])
           + _dot(at_ref[r, :], wo_ref[FOURIER_WIDTH:, :]) for f, r in zip(fms, rows)]
    x1 = [x_ref[r, :] + gt1_ref[...] * m for r, m in zip(rows, mix)]
    h2 = [(_rms(x) * gffn_ref[...] * (1.0 + sc2_ref[...]) + sh2_ref[...]).astype(BF16) for x in x1]
    d_ff = wg_ref.shape[1]
    ffn = [None] * row_groups
    for c0 in range(0, d_ff, ff_chunk):
        gate = [_dot(h, wg_ref[:, c0:c0 + ff_chunk]) for h in h2]
        up = [_dot(h, wu_ref[:, c0:c0 + ff_chunk]) for h in h2]
        act = [(g / (1.0 + jnp.exp(-g)) * u).astype(BF16) for g, u in zip(gate, up)]
        for i, a in enumerate(act):
            part = _dot(a, wd_ref[c0:c0 + ff_chunk, :])
            ffn[i] = part if ffn[i] is None else ffn[i] + part
    for r, x, f in zip(rows, x1, ffn):
        o_ref[r, :] = _rms(x + gt2_ref[...] * f) * gfin_ref[...]


def _ffn(x, fm, at, gt1, sh2, sc2, gt2, g_ffn, g_final, w_out, w_gate, w_up, w_down, tm, ff_chunk):
    b, s, d = x.shape
    d_ff = w_gate.shape[1]
    n1 = s // DFT_INNER
    fm = fm.reshape(b, DFT_INNER, n1 + SUBLANES, FOURIER_WIDTH)
    vec = pl.BlockSpec((None, 1, d), lambda bi, i: (bi, 0, 0))
    single = pl.Buffered(1)
    const = lambda shape: pl.BlockSpec(shape, lambda bi, i: (0,) * len(shape))
    weight = lambda shape: pl.BlockSpec(shape, lambda bi, i: (0,) * len(shape), pipeline_mode=single)
    return pl.pallas_call(
        functools.partial(_ffn_kernel, ff_chunk=ff_chunk, row_groups=2),
        grid=(b, s // tm),
        in_specs=[pl.BlockSpec((None, tm, d), lambda bi, i: (bi, i, 0)),
                  pl.BlockSpec((None, tm // n1, n1 + SUBLANES, FOURIER_WIDTH),
                               lambda bi, i: (bi, i, 0, 0)),
                  pl.BlockSpec((None, tm, ATTN_WIDTH), lambda bi, i: (bi, i, 0)),
                  vec, vec, vec, vec, const((1, d)), const((1, d)),
                  weight((d, d)), weight((d, d_ff)), weight((d, d_ff)), weight((d_ff, d))],
        out_specs=pl.BlockSpec((None, tm, d), lambda bi, i: (bi, i, 0)),
        out_shape=jax.ShapeDtypeStruct((b, s, d), F32),
        compiler_params=pltpu.CompilerParams(dimension_semantics=("parallel", "parallel"),
                                             vmem_limit_bytes=VMEM_LIMIT),
        name="ffn",
    )(x, fm, at, gt1, sh2, sc2, gt2, g_ffn, g_final, w_out, w_gate, w_up, w_down)


def _rope_tables(n_tokens):
    n_freq = HEAD_DIM // 4
    inv_freq = ROPE_THETA ** (-np.arange(n_freq, dtype=np.float64) / n_freq)
    t = np.arange(n_tokens)
    row_ang = (t // GRID_W).astype(np.float64)[:, None] * inv_freq[None, :]
    col_ang = (t % GRID_W).astype(np.float64)[:, None] * inv_freq[None, :]
    ang = np.concatenate([row_ang, row_ang, col_ang, col_ang], axis=1)
    sign = np.tile(np.concatenate([-np.ones(n_freq), np.ones(n_freq)]), 2)
    cos = np.cos(ang)
    sin = np.sin(ang) * sign[None, :]
    reps = LANES // HEAD_DIM
    return (np.tile(cos, (1, reps)).astype(np.float32), np.tile(sin, (1, reps)).astype(np.float32))


def _head_block_diag(width):
    idx = np.arange(width) // HEAD_DIM
    return (idx[:, None] == idx[None, :]).astype(np.float32)


def _key_block(n_keys):
    for tk in range(256, 0, -LANES):
        if n_keys % tk == 0:
            return tk
    raise ValueError(f"n_keys={n_keys} is not a multiple of {LANES}")


def kernel(x, c, ctx, c_ctx, w_ada, b_ada, g_mix, w_in, w_four, q_gain, k_gain, w_out, g_ffn,
           w_gate, w_up, w_down, g_final):
    b, s, d = x.shape
    n_ctx = ctx.shape[1]
    assert w_ada.shape[0] == 1, "single-layer block"
    assert s % (DFT_INNER * SUBLANES) == 0 and n_ctx % LANES == 0

    pad = (-(b + 1)) % SUBLANES
    cond = jnp.concatenate([c, c_ctx[None, :], jnp.zeros((pad, d), F32)], axis=0)
    ada = _adaln(cond, w_ada[0], b_ada[0][None, :])
    sh1, sc1, gt1, sh2, sc2, gt2 = [ada[:, i * d:(i + 1) * d] for i in range(6)]
    per_batch = lambda m: m[:b, None, :]
    for_ctx = lambda m: jnp.broadcast_to(m[b][None, None, :], (b, 1, d))

    w_in_b = w_in[0].astype(BF16)
    bd = jnp.asarray(_head_block_diag(ATTN_WIDTH)).astype(BF16)
    qg = jnp.tile(q_gain[0], N_HEADS)[None, :]
    kg = jnp.tile(k_gain[0], N_KV_HEADS)[None, :]
    g_mix2 = g_mix[0][None, :]

    cos, sin = _rope_tables(s)
    tm_proj = min(512, s)
    u, qt, k, vt = _project(x, per_batch(sh1), per_batch(sc1), g_mix2, w_in_b, qg, kg, bd,
                            jnp.asarray(cos), jnp.asarray(sin), tm_proj)
    _, _, kc, vtc = _project(ctx, for_ctx(sh1), for_ctx(sc1), g_mix2, w_in_b, qg, kg, bd,
                             jnp.ones((n_ctx, LANES), F32), jnp.zeros((n_ctx, LANES), F32), n_ctx)
    n_keys = s + n_ctx
    k_all = jnp.concatenate([k, kc], axis=1)
    vt_all = jnp.concatenate([vt, vtc], axis=2).reshape(b, N_KV_HEADS, HEAD_DIM, n_keys)
    ones_rows = jnp.zeros((b, N_KV_HEADS, VT_ROWS - HEAD_DIM, n_keys), BF16).at[:, :, 0, :].set(1.0)
    vt_aug = jnp.concatenate([vt_all, ones_rows], axis=2).reshape(b, N_KV_HEADS * VT_ROWS, n_keys)

    ma, mb, mc = _fourier_tables(s)
    fm = _fourier(u, jnp.asarray(ma).astype(BF16), jnp.asarray(mb).astype(BF16),
                  jnp.asarray(mc), w_four[0], unroll=8)

    at = _attention(qt, k_all, vt_aug, q_gain, k_gain, tq=min(512, s), tk=_key_block(n_keys), unroll=True)

    tm_ffn = min(512, s)
    d_ff = w_gate.shape[2]
    ff_chunk = d_ff
    return _ffn(x, fm, at, per_batch(gt1), per_batch(sh2), per_batch(sc2), per_batch(gt2),
                g_ffn[0][None, :], g_final[None, :], w_out[0].astype(BF16), w_gate[0].astype(BF16),
                w_up[0].astype(BF16), w_down[0].astype(BF16), tm_ffn, ff_chunk)
```

```python
import functools
import math

import numpy as np
import jax
import jax.numpy as jnp
from jax import lax
from jax.experimental import pallas as pl
from jax.experimental.pallas import tpu as pltpu

F32 = jnp.float32
BF16 = jnp.bfloat16

EPS = 1e-6
ROPE_THETA = 10000.0
GRID_W = 64
HEAD_DIM = 64
N_HEADS = 8
N_KV_HEADS = 2
Q_PER_KV = N_HEADS // N_KV_HEADS
N_FOURIER_GROUPS = 4
FOURIER_GROUP_DIM = 128
FOURIER_WIDTH = N_FOURIER_GROUPS * FOURIER_GROUP_DIM
ATTN_WIDTH = N_HEADS * HEAD_DIM
KV_WIDTH = N_KV_HEADS * HEAD_DIM

LANES = 128
MXU_COLS = 256
SUBLANES = 8
DFT_INNER = 128
ROW_PITCH = DFT_INNER + SUBLANES
VMEM_LIMIT = 56 * 1024 * 1024

Q_SCALE = (HEAD_DIM ** -0.5) * math.log2(math.e)
BF16_SUBLANES = 16
VT_ROWS = HEAD_DIM + BF16_SUBLANES
SCORE_BOUND_SLACK = 1.02
MAX_UNSHIFTED_SCORE = 64.0


def _dot(a, b):
    return jnp.dot(a, b, preferred_element_type=F32)


def _split_bf16(a):
    hi = a.astype(BF16)
    return hi, (a - hi.astype(F32)).astype(BF16)


def _rms(x):
    return x * lax.rsqrt(jnp.mean(x * x, axis=-1, keepdims=True) + EPS)


def _adaln_kernel(cond_ref, w_ref, b_ref, o_ref):
    c = cond_ref[...]
    s_hi, s_lo = _split_bf16(c / (1.0 + jnp.exp(-c)))
    w_hi, w_lo = _split_bf16(w_ref[...])
    o_ref[...] = _dot(s_hi, w_hi) + _dot(s_lo, w_hi) + _dot(s_hi, w_lo) + b_ref[...]


def _adaln(cond, w, b):
    rows, d = cond.shape
    n = w.shape[1]
    tn = n // 4
    return pl.pallas_call(
        _adaln_kernel,
        grid=(n // tn,),
        in_specs=[pl.BlockSpec((rows, d), lambda j: (0, 0)),
                  pl.BlockSpec((d, tn), lambda j: (0, j)),
                  pl.BlockSpec((1, tn), lambda j: (0, j))],
        out_specs=pl.BlockSpec((rows, tn), lambda j: (0, j)),
        out_shape=jax.ShapeDtypeStruct((rows, n), F32),
        compiler_params=pltpu.CompilerParams(dimension_semantics=("arbitrary",),
                                             vmem_limit_bytes=VMEM_LIMIT),
        name="adaln",
    )(cond, w, b)


def _proj_kernel(x_ref, sh_ref, sc_ref, g_ref, w_ref, qg_ref, kg_ref, bd_ref, cos_ref, sin_ref,
                 u_ref, qt_ref, k_ref, vt_ref):
    tm = x_ref.shape[0]
    h = _rms(x_ref[...]) * g_ref[...]
    h = h * (1.0 + sc_ref[...]) + sh_ref[...]
    p = _dot(h.astype(BF16), w_ref[...])
    o1 = FOURIER_WIDTH
    o2 = o1 + ATTN_WIDTH
    o3 = o2 + KV_WIDTH
    u_ref[...] = p[:, :o1].astype(u_ref.dtype)
    vt_ref[...] = p[:, o3:].T.astype(vt_ref.dtype)

    cos = cos_ref[...]
    sin = sin_ref[...]
    lane = lax.broadcasted_iota(jnp.int32, (tm, LANES), 1)
    first_half = (lane & 31) < 16

    def norm_rope(t, gain, bd):
        hi, lo = _split_bf16(t * t)
        ss = _dot(hi, bd) + _dot(lo, bd)
        tn = t * lax.rsqrt(ss * (1.0 / HEAD_DIM) + EPS) * gain
        outs = []
        for j in range(t.shape[1] // LANES):
            tj = tn[:, j * LANES:(j + 1) * LANES]
            partner = jnp.where(first_half,
                                pltpu.roll(tj, LANES - 16, axis=1),
                                pltpu.roll(tj, 16, axis=1))
            outs.append(tj * cos + partner * sin)
        return outs

    q_slabs = norm_rope(p[:, o1:o2], qg_ref[...], bd_ref[...])
    for j, qs in enumerate(q_slabs):
        qt_ref[j * LANES:(j + 1) * LANES, :] = (qs * Q_SCALE).T.astype(qt_ref.dtype)
    (k_rot,) = norm_rope(p[:, o2:o3], kg_ref[...], bd_ref[:KV_WIDTH, :KV_WIDTH])
    k_ref[...] = k_rot.astype(k_ref.dtype)


def _project(x, shift, scale, gain, w_in, q_gain, k_gain, bd, cos, sin, tm):
    b, s, d = x.shape
    in_w = w_in.shape[1]
    vec = pl.BlockSpec((None, 1, d), lambda bi, i: (bi, 0, 0))
    const = lambda shape: pl.BlockSpec(shape, lambda bi, i: (0,) * len(shape))
    return pl.pallas_call(
        _proj_kernel,
        grid=(b, s // tm),
        in_specs=[pl.BlockSpec((None, tm, d), lambda bi, i: (bi, i, 0)),
                  vec, vec, const((1, d)), const((d, in_w)),
                  const((1, ATTN_WIDTH)), const((1, KV_WIDTH)), const((ATTN_WIDTH, ATTN_WIDTH)),
                  pl.BlockSpec((tm, LANES), lambda bi, i: (i, 0)),
                  pl.BlockSpec((tm, LANES), lambda bi, i: (i, 0))],
        out_specs=[pl.BlockSpec((None, tm, FOURIER_WIDTH), lambda bi, i: (bi, i, 0)),
                   pl.BlockSpec((None, ATTN_WIDTH, tm), lambda bi, i: (bi, 0, i)),
                   pl.BlockSpec((None, tm, KV_WIDTH), lambda bi, i: (bi, i, 0)),
                   pl.BlockSpec((None, KV_WIDTH, tm), lambda bi, i: (bi, 0, i))],
        out_shape=[jax.ShapeDtypeStruct((b, s, FOURIER_WIDTH), BF16),
                   jax.ShapeDtypeStruct((b, ATTN_WIDTH, s), BF16),
                   jax.ShapeDtypeStruct((b, s, KV_WIDTH), BF16),
                   jax.ShapeDtypeStruct((b, KV_WIDTH, s), BF16)],
        compiler_params=pltpu.CompilerParams(dimension_semantics=("parallel", "parallel"),
                                             vmem_limit_bytes=VMEM_LIMIT),
        name="proj",
    )(x, shift, scale, gain, w_in, q_gain, k_gain, bd, cos, sin)


def _fourier_kernel(u_ref, ma_ref, mb_ref, mc_ref, wf_ref, o_ref, zr_sc, zi_sc, tsc, *, unroll):
    n2_count = DFT_INNER
    n1_count = u_ref.shape[0] // n2_count
    c = FOURIER_GROUP_DIM
    t_rows = 2 * n1_count
    t_pitch = t_rows + SUBLANES
    o_pitch = n1_count + SUBLANES

    fw = _dot(mc_ref[...], wf_ref[...]).astype(BF16)
    fw_cat = jnp.concatenate([fw[:c], fw[c:]], axis=1)

    def fill(i, carry):
        src = pl.multiple_of(i * n2_count, n2_count)
        dst = pl.multiple_of(i * ROW_PITCH, SUBLANES)
        z = _dot(u_ref[pl.ds(src, n2_count), :], fw_cat)
        zr_sc[pl.ds(dst, n2_count), :] = z[:, :c]
        zi_sc[pl.ds(dst, n2_count), :] = z[:, c:]
        return carry
    lax.fori_loop(0, n1_count, fill, 0, unroll=min(unroll, n1_count))

    def stage_a(n2, carry):
        zr = zr_sc[pl.ds(n2, n1_count, stride=ROW_PITCH), :]
        zi = zi_sc[pl.ds(n2, n1_count, stride=ROW_PITCH), :]
        z2 = jnp.concatenate([zr, zi], axis=0).astype(BF16)
        dst = pl.multiple_of(n2 * t_pitch, SUBLANES)
        tsc[pl.ds(dst, t_rows), :] = _dot(ma_ref[n2], z2)
        return carry
    lax.fori_loop(0, n2_count, stage_a, 0, unroll=min(2 * unroll, n2_count))

    def stage_b(k1, carry):
        tr = tsc[pl.ds(k1, n2_count, stride=t_pitch), :]
        ti = tsc[pl.ds(n1_count + k1, n2_count, stride=t_pitch), :]
        t2 = jnp.concatenate([tr, ti], axis=0).astype(BF16)
        o_ref[pl.ds(k1, n2_count, stride=o_pitch), :] = _dot(mb_ref[...], t2)
        return carry
    lax.fori_loop(0, n1_count, stage_b, 0, unroll=min(unroll, n1_count))
    for pad_row in range(SUBLANES):
        o_ref[pl.ds(n1_count + pad_row, n2_count, stride=o_pitch), :] = jnp.zeros((n2_count, c), F32)


def _fourier(u, ma, mb, mc, w_four, unroll):
    b, s, _ = u.shape
    c = FOURIER_GROUP_DIM
    n1 = s // DFT_INNER
    single = pl.Buffered(1)
    return pl.pallas_call(
        functools.partial(_fourier_kernel, unroll=unroll),
        grid=(b, N_FOURIER_GROUPS),
        in_specs=[pl.BlockSpec((None, s, c), lambda bi, g: (bi, 0, g)),
                  pl.BlockSpec(ma.shape, lambda bi, g: (0, 0, 0), pipeline_mode=single),
                  pl.BlockSpec(mb.shape, lambda bi, g: (0, 0), pipeline_mode=single),
                  pl.BlockSpec(mc.shape, lambda bi, g: (0, 0), pipeline_mode=single),
                  pl.BlockSpec((None, c, c), lambda bi, g: (g, 0, 0))],
        out_specs=pl.BlockSpec((None, DFT_INNER * (n1 + SUBLANES), c), lambda bi, g: (bi, 0, g)),
        out_shape=jax.ShapeDtypeStruct((b, DFT_INNER * (n1 + SUBLANES), FOURIER_WIDTH), F32),
        scratch_shapes=[pltpu.VMEM((n1 * ROW_PITCH, c), F32),
                        pltpu.VMEM((n1 * ROW_PITCH, c), F32),
                        pltpu.VMEM((DFT_INNER * (2 * n1 + SUBLANES), c), F32)],
        compiler_params=pltpu.CompilerParams(dimension_semantics=("parallel", "parallel"),
                                             vmem_limit_bytes=VMEM_LIMIT),
        name="fourier",
    )(u, ma, mb, mc, w_four)


def _fourier_tables(s):
    n2c = DFT_INNER
    n1c = s // n2c
    k1 = np.arange(n1c, dtype=np.float64)
    n = (n2c * np.arange(n1c)[None, :] + np.arange(n2c)[:, None]).astype(np.float64)
    ang = 2.0 * np.pi * k1[None, :, None] * n[:, None, :] / s
    ca, sa = np.cos(ang), np.sin(ang)
    ma = np.concatenate([np.concatenate([ca, sa], axis=2),
                         np.concatenate([-sa, ca], axis=2)], axis=1)
    kk = np.arange(n2c, dtype=np.float64)
    a2 = 2.0 * np.pi * np.outer(kk, kk) / n2c
    mb = np.concatenate([np.cos(a2), np.sin(a2)], axis=1)
    cc = np.arange(FOURIER_GROUP_DIM, dtype=np.float64)
    ac = 2.0 * np.pi * np.outer(cc, cc) / FOURIER_GROUP_DIM
    norm = 1.0 / math.sqrt(s * FOURIER_GROUP_DIM)
    mc = np.concatenate([np.cos(ac), -np.sin(ac)], axis=0) * norm
    return (ma.astype(np.float32), mb.astype(np.float32), mc.astype(np.float32))


def _attn_kernel(bound_ref, qt_ref, k_ref, vt_ref, o_ref, acc_ref, p_ref, *, tk, unroll):
    tq = qt_ref.shape[1]
    n_keys = k_ref.shape[0]
    g = pl.program_id(1)
    parts = []
    for h in range(Q_PER_KV):
        qh = qt_ref[h * HEAD_DIM:(h + 1) * HEAD_DIM, :]
        z = jnp.zeros_like(qh)
        parts.append(jnp.where(g == 0, jnp.concatenate([qh, z], axis=0),
                               jnp.concatenate([z, qh], axis=0)))
    w = jnp.concatenate(parts, axis=1)
    rq = w.shape[1]
    acc_ref[...] = jnp.zeros(acc_ref.shape, F32)

    unshifted_ok = bound_ref[0] <= MAX_UNSHIFTED_SCORE

    def blocks(j):
        start = pl.multiple_of(j * tk, tk)
        return k_ref[pl.ds(start, tk), :], vt_ref[:, pl.ds(start, tk)]

    @pl.when(unshifted_ok)
    def _():
        kb0, _ = blocks(0)
        p_ref[...] = jnp.exp2(_dot(kb0, w)).astype(BF16)

        def body(j, carry):
            kb, _ = blocks(j)
            _, vb_prev = blocks(j - 1)
            for c0 in range(0, rq, MXU_COLS):
                cols = slice(c0, c0 + MXU_COLS)
                s = _dot(kb, w[:, cols])
                acc_ref[:, cols] += _dot(vb_prev, p_ref[:, cols])
                p_ref[:, cols] = jnp.exp2(s).astype(BF16)
            return carry
        n_blocks = n_keys // tk
        lax.fori_loop(1, n_blocks, body, 0, unroll=unroll)
        _, vb_last = blocks(n_blocks - 1)
        acc_ref[...] += _dot(vb_last, p_ref[...])

    @pl.when(jnp.logical_not(unshifted_ok))
    def _():
        def body(j, m_prev):
            kb, vb = blocks(j)
            s = _dot(kb, w)
            m_next = jnp.maximum(m_prev, jnp.max(s, axis=0, keepdims=True))
            alpha = jnp.exp2(m_prev - m_next)
            p = jnp.exp2(s - m_next).astype(BF16)
            acc_ref[...] = alpha * acc_ref[...] + _dot(vb, p)
            return m_next
        lax.fori_loop(0, n_keys // tk, body, jnp.full((1, rq), -jnp.inf, F32))

    acc = acc_ref[...]
    out_t = acc[:HEAD_DIM] / acc[HEAD_DIM:HEAD_DIM + 1]
    out_t = jnp.concatenate([out_t[:, h * tq:(h + 1) * tq] for h in range(Q_PER_KV)], axis=0)
    o_ref[...] = out_t.T.astype(o_ref.dtype)


def _attention(qt, k, vt_aug, q_gain, k_gain, tq, tk, unroll):
    b, _, s = qt.shape
    n_keys = k.shape[1]
    gw = Q_PER_KV * HEAD_DIM
    bound = (jnp.max(jnp.abs(q_gain)) * jnp.max(jnp.abs(k_gain))
             * (HEAD_DIM * Q_SCALE * SCORE_BOUND_SLACK)).reshape(1).astype(F32)
    return pl.pallas_call(
        functools.partial(_attn_kernel, tk=tk, unroll=unroll),
        grid=(b, N_KV_HEADS, s // tq),
        in_specs=[pl.BlockSpec(memory_space=pltpu.SMEM),
                  pl.BlockSpec((None, gw, tq), lambda bi, g, i: (bi, g, i)),
                  pl.BlockSpec((None, n_keys, KV_WIDTH), lambda bi, g, i: (bi, 0, 0)),
                  pl.BlockSpec((None, VT_ROWS, n_keys), lambda bi, g, i: (bi, g, 0))],
        out_specs=pl.BlockSpec((None, tq, gw), lambda bi, g, i: (bi, i, g)),
        out_shape=jax.ShapeDtypeStruct((b, s, ATTN_WIDTH), BF16),
        scratch_shapes=[pltpu.VMEM((VT_ROWS, Q_PER_KV * tq), F32),
                        pltpu.VMEM((tk, Q_PER_KV * tq), BF16)],
        compiler_params=pltpu.CompilerParams(
            dimension_semantics=("parallel", "parallel", "parallel"),
            vmem_limit_bytes=VMEM_LIMIT),
        name="attention",
    )(bound, qt, k, vt_aug)


def _ffn_kernel(x_ref, fm_ref, at_ref, gt1_ref, sh2_ref, sc2_ref, gt2_ref, gffn_ref, gfin_ref,
                wo_ref, wg_ref, wu_ref, wd_ref, o_ref, *, ff_chunk, row_groups):
    tm = x_ref.shape[0]
    gm = tm // row_groups
    rows = [slice(i * gm, (i + 1) * gm) for i in range(row_groups)]
    n1 = fm_ref.shape[1] - SUBLANES
    k2g = gm // n1
    fms = [fm_ref[i * k2g:(i + 1) * k2g, :n1, :].reshape(gm, FOURIER_WIDTH) for i in range(row_groups)]
    mix = [_dot(f.astype(BF16), wo_ref[:FOURIER_WIDTH, :])
           + _dot(at_ref[r, :], wo_ref[FOURIER_WIDTH:, :]) for f, r in zip(fms, rows)]
    x1 = [x_ref[r, :] + gt1_ref[...] * m for r, m in zip(rows, mix)]
    h2 = [(_rms(x) * gffn_ref[...] * (1.0 + sc2_ref[...]) + sh2_ref[...]).astype(BF16) for x in x1]
    d_ff = wg_ref.shape[1]
    ffn = [None] * row_groups
    for c0 in range(0, d_ff, ff_chunk):
        gate = [_dot(h, wg_ref[:, c0:c0 + ff_chunk]) for h in h2]
        up = [_dot(h, wu_ref[:, c0:c0 + ff_chunk]) for h in h2]
        act = [(g / (1.0 + jnp.exp(-g)) * u).astype(BF16) for g, u in zip(gate, up)]
        for i, a in enumerate(act):
            part = _dot(a, wd_ref[c0:c0 + ff_chunk, :])
            ffn[i] = part if ffn[i] is None else ffn[i] + part
    for r, x, f in zip(rows, x1, ffn):
        o_ref[r, :] = _rms(x + gt2_ref[...] * f) * gfin_ref[...]


def _ffn(x, fm, at, gt1, sh2, sc2, gt2, g_ffn, g_final, w_out, w_gate, w_up, w_down, tm, ff_chunk):
    b, s, d = x.shape
    d_ff = w_gate.shape[1]
    n1 = s // DFT_INNER
    fm = fm.reshape(b, DFT_INNER, n1 + SUBLANES, FOURIER_WIDTH)
    vec = pl.BlockSpec((None, 1, d), lambda bi, i: (bi, 0, 0))
    single = pl.Buffered(1)
    const = lambda shape: pl.BlockSpec(shape, lambda bi, i: (0,) * len(shape))
    weight = lambda shape: pl.BlockSpec(shape, lambda bi, i: (0,) * len(shape), pipeline_mode=single)
    return pl.pallas_call(
        functools.partial(_ffn_kernel, ff_chunk=ff_chunk, row_groups=2),
        grid=(b, s // tm),
        in_specs=[pl.BlockSpec((None, tm, d), lambda bi, i: (bi, i, 0)),
                  pl.BlockSpec((None, tm // n1, n1 + SUBLANES, FOURIER_WIDTH),
                               lambda bi, i: (bi, i, 0, 0)),
                  pl.BlockSpec((None, tm, ATTN_WIDTH), lambda bi, i: (bi, i, 0)),
                  vec, vec, vec, vec, const((1, d)), const((1, d)),
                  weight((d, d)), weight((d, d_ff)), weight((d, d_ff)), weight((d_ff, d))],
        out_specs=pl.BlockSpec((None, tm, d), lambda bi, i: (bi, i, 0)),
        out_shape=jax.ShapeDtypeStruct((b, s, d), F32),
        compiler_params=pltpu.CompilerParams(dimension_semantics=("parallel", "parallel"),
                                             vmem_limit_bytes=VMEM_LIMIT),
        name="ffn",
    )(x, fm, at, gt1, sh2, sc2, gt2, g_ffn, g_final, w_out, w_gate, w_up, w_down)


def _rope_tables(n_tokens):
    n_freq = HEAD_DIM // 4
    inv_freq = ROPE_THETA ** (-np.arange(n_freq, dtype=np.float64) / n_freq)
    t = np.arange(n_tokens)
    row_ang = (t // GRID_W).astype(np.float64)[:, None] * inv_freq[None, :]
    col_ang = (t % GRID_W).astype(np.float64)[:, None] * inv_freq[None, :]
    ang = np.concatenate([row_ang, row_ang, col_ang, col_ang], axis=1)
    sign = np.tile(np.concatenate([-np.ones(n_freq), np.ones(n_freq)]), 2)
    cos = np.cos(ang)
    sin = np.sin(ang) * sign[None, :]
    reps = LANES // HEAD_DIM
    return (np.tile(cos, (1, reps)).astype(np.float32), np.tile(sin, (1, reps)).astype(np.float32))


def _head_block_diag(width):
    idx = np.arange(width) // HEAD_DIM
    return (idx[:, None] == idx[None, :]).astype(np.float32)


def _key_block(n_keys):
    for tk in range(256, 0, -LANES):
        if n_keys % tk == 0:
            return tk
    raise ValueError(f"n_keys={n_keys} is not a multiple of {LANES}")


def kernel(x, c, ctx, c_ctx, w_ada, b_ada, g_mix, w_in, w_four, q_gain, k_gain, w_out, g_ffn,
           w_gate, w_up, w_down, g_final):
    b, s, d = x.shape
    n_ctx = ctx.shape[1]
    assert w_ada.shape[0] == 1, "single-layer block"
    assert s % (DFT_INNER * SUBLANES) == 0 and n_ctx % LANES == 0

    pad = (-(b + 1)) % SUBLANES
    cond = jnp.concatenate([c, c_ctx[None, :], jnp.zeros((pad, d), F32)], axis=0)
    ada = _adaln(cond, w_ada[0], b_ada[0][None, :])
    sh1, sc1, gt1, sh2, sc2, gt2 = [ada[:, i * d:(i + 1) * d] for i in range(6)]
    per_batch = lambda m: m[:b, None, :]
    for_ctx = lambda m: jnp.broadcast_to(m[b][None, None, :], (b, 1, d))

    w_in_b = w_in[0].astype(BF16)
    bd = jnp.asarray(_head_block_diag(ATTN_WIDTH)).astype(BF16)
    qg = jnp.tile(q_gain[0], N_HEADS)[None, :]
    kg = jnp.tile(k_gain[0], N_KV_HEADS)[None, :]
    g_mix2 = g_mix[0][None, :]

    cos, sin = _rope_tables(s)
    tm_proj = min(512, s)
    u, qt, k, vt = _project(x, per_batch(sh1), per_batch(sc1), g_mix2, w_in_b, qg, kg, bd,
                            jnp.asarray(cos), jnp.asarray(sin), tm_proj)
    _, _, kc, vtc = _project(ctx, for_ctx(sh1), for_ctx(sc1), g_mix2, w_in_b, qg, kg, bd,
                             jnp.ones((n_ctx, LANES), F32), jnp.zeros((n_ctx, LANES), F32), n_ctx)
    n_keys = s + n_ctx
    k_all = jnp.concatenate([k, kc], axis=1)
    vt_all = jnp.concatenate([vt, vtc], axis=2).reshape(b, N_KV_HEADS, HEAD_DIM, n_keys)
    ones_rows = jnp.zeros((b, N_KV_HEADS, VT_ROWS - HEAD_DIM, n_keys), BF16).at[:, :, 0, :].set(1.0)
    vt_aug = jnp.concatenate([vt_all, ones_rows], axis=2).reshape(b, N_KV_HEADS * VT_ROWS, n_keys)

    ma, mb, mc = _fourier_tables(s)
    fm = _fourier(u, jnp.asarray(ma).astype(BF16), jnp.asarray(mb).astype(BF16),
                  jnp.asarray(mc), w_four[0], unroll=8)

    at = _attention(qt, k_all, vt_aug, q_gain, k_gain, tq=min(512, s), tk=_key_block(n_keys), unroll=True)

    tm_ffn = min(512, s)
    d_ff = w_gate.shape[2]
    ff_chunk = d_ff
    return _ffn(x, fm, at, per_batch(gt1), per_batch(sh2), per_batch(sc2), per_batch(gt2),
                g_ffn[0][None, :], g_final[None, :], w_out[0].astype(BF16), w_gate[0].astype(BF16),
                w_up[0].astype(BF16), w_down[0].astype(BF16), tm_ffn, ff_chunk)
```

```python
import functools
import math

import numpy as np
import jax
import jax.numpy as jnp
from jax import lax
from jax.experimental import pallas as pl
from jax.experimental.pallas import tpu as pltpu

F32 = jnp.float32
BF16 = jnp.bfloat16

EPS = 1e-6
ROPE_THETA = 10000.0
GRID_W = 64
HEAD_DIM = 64
N_HEADS = 8
N_KV_HEADS = 2
Q_PER_KV = N_HEADS // N_KV_HEADS
N_FOURIER_GROUPS = 4
FOURIER_GROUP_DIM = 128
FOURIER_WIDTH = N_FOURIER_GROUPS * FOURIER_GROUP_DIM
ATTN_WIDTH = N_HEADS * HEAD_DIM
KV_WIDTH = N_KV_HEADS * HEAD_DIM

LANES = 128
MXU_COLS = 256
SUBLANES = 8
DFT_INNER = 128
ROW_PITCH = DFT_INNER + SUBLANES
VMEM_LIMIT = 56 * 1024 * 1024

Q_SCALE = (HEAD_DIM ** -0.5) * math.log2(math.e)
SCORE_BOUND_SLACK = 1.02
MAX_UNSHIFTED_SCORE = 64.0


def _dot(a, b):
    return jnp.dot(a, b, preferred_element_type=F32)


def _split_bf16(a):
    hi = a.astype(BF16)
    return hi, (a - hi.astype(F32)).astype(BF16)


def _rms(x):
    return x * lax.rsqrt(jnp.mean(x * x, axis=-1, keepdims=True) + EPS)


def _adaln_kernel(cond_ref, w_ref, b_ref, o_ref):
    c = cond_ref[...]
    s_hi, s_lo = _split_bf16(c / (1.0 + jnp.exp(-c)))
    w_hi, w_lo = _split_bf16(w_ref[...])
    o_ref[...] = _dot(s_hi, w_hi) + _dot(s_lo, w_hi) + _dot(s_hi, w_lo) + b_ref[...]


def _adaln(cond, w, b):
    rows, d = cond.shape
    n = w.shape[1]
    tn = n // 4
    return pl.pallas_call(
        _adaln_kernel,
        grid=(n // tn,),
        in_specs=[pl.BlockSpec((rows, d), lambda j: (0, 0)),
                  pl.BlockSpec((d, tn), lambda j: (0, j)),
                  pl.BlockSpec((1, tn), lambda j: (0, j))],
        out_specs=pl.BlockSpec((rows, tn), lambda j: (0, j)),
        out_shape=jax.ShapeDtypeStruct((rows, n), F32),
        compiler_params=pltpu.CompilerParams(dimension_semantics=("arbitrary",),
                                             vmem_limit_bytes=VMEM_LIMIT),
        name="adaln",
    )(cond, w, b)


def _proj_kernel(x_ref, sh_ref, sc_ref, g_ref, w_ref, qgt_ref, kg_ref, bd_ref, cos_ref, sin_ref,
                 cost_ref, sint_ref, u_ref, qt_ref, k_ref, vt_ref, *, row_groups):
    tm = x_ref.shape[0]
    gm = tm // row_groups
    rows = [slice(i * gm, (i + 1) * gm) for i in range(row_groups)]
    o1 = FOURIER_WIDTH
    o2 = o1 + ATTN_WIDTH
    o3 = o2 + KV_WIDTH
    hs = [((_rms(x_ref[r, :]) * g_ref[...]) * (1.0 + sc_ref[...]) + sh_ref[...]).astype(BF16)
          for r in rows]
    ps = [_dot(h, w_ref[...]) for h in hs]

    lane = lax.broadcasted_iota(jnp.int32, (gm, LANES), 1)
    first_half = (lane & 31) < 16
    gain_t = jnp.tile(qgt_ref[...], (1, gm // LANES))
    quarter = HEAD_DIM // 4

    for r, p in zip(rows, ps):
        u_ref[r, :] = p[:, :o1].astype(u_ref.dtype)
        vt_ref[:, r] = p[:, o3:].T.astype(vt_ref.dtype)

        kx = p[:, o2:o3]
        hi, lo = _split_bf16(kx * kx)
        ss = _dot(hi, bd_ref[...]) + _dot(lo, bd_ref[...])
        kn = kx * lax.rsqrt(ss * (1.0 / HEAD_DIM) + EPS) * kg_ref[...]
        partner = jnp.where(first_half, pltpu.roll(kn, LANES - quarter, axis=1),
                            pltpu.roll(kn, quarter, axis=1))
        k_ref[r, :] = (kn * cos_ref[r, :] + partner * sin_ref[r, :]).astype(k_ref.dtype)

        cos_t = cost_ref[:, r]
        sin_t = sint_ref[:, r]
        for j in range(ATTN_WIDTH // LANES):
            qtj = p[:, o1 + j * LANES:o1 + (j + 1) * LANES].T
            for hh in range(LANES // HEAD_DIM):
                qh = qtj[hh * HEAD_DIM:(hh + 1) * HEAD_DIM]
                ss = jnp.sum(qh * qh, axis=0, keepdims=True)
                qn = qh * lax.rsqrt(ss * (1.0 / HEAD_DIM) + EPS) * gain_t
                partner = jnp.concatenate([qn[quarter:2 * quarter], qn[:quarter],
                                           qn[3 * quarter:], qn[2 * quarter:3 * quarter]], axis=0)
                row0 = j * LANES + hh * HEAD_DIM
                qt_ref[row0:row0 + HEAD_DIM, r] = (qn * cos_t + partner * sin_t).astype(qt_ref.dtype)


def _project(x, shift, scale, gain, w_in, q_gain_t, k_gain, bd, cos, sin, cos_t, sin_t, tm):
    b, s, d = x.shape
    in_w = w_in.shape[1]
    vec = pl.BlockSpec((None, 1, d), lambda bi, i: (bi, 0, 0))
    const = lambda shape: pl.BlockSpec(shape, lambda bi, i: (0,) * len(shape))
    row_groups = 2 if tm % (2 * MXU_COLS) == 0 else 1
    return pl.pallas_call(
        functools.partial(_proj_kernel, row_groups=row_groups),
        grid=(b, s // tm),
        in_specs=[pl.BlockSpec((None, tm, d), lambda bi, i: (bi, i, 0)),
                  vec, vec, const((1, d)), const((d, in_w)),
                  const((HEAD_DIM, LANES)), const((1, KV_WIDTH)), const((KV_WIDTH, KV_WIDTH)),
                  pl.BlockSpec((tm, LANES), lambda bi, i: (i, 0)),
                  pl.BlockSpec((tm, LANES), lambda bi, i: (i, 0)),
                  pl.BlockSpec((HEAD_DIM, tm), lambda bi, i: (0, i)),
                  pl.BlockSpec((HEAD_DIM, tm), lambda bi, i: (0, i))],
        out_specs=[pl.BlockSpec((None, tm, FOURIER_WIDTH), lambda bi, i: (bi, i, 0)),
                   pl.BlockSpec((None, ATTN_WIDTH, tm), lambda bi, i: (bi, 0, i)),
                   pl.BlockSpec((None, tm, KV_WIDTH), lambda bi, i: (bi, i, 0)),
                   pl.BlockSpec((None, KV_WIDTH, tm), lambda bi, i: (bi, 0, i))],
        out_shape=[jax.ShapeDtypeStruct((b, s, FOURIER_WIDTH), BF16),
                   jax.ShapeDtypeStruct((b, ATTN_WIDTH, s), BF16),
                   jax.ShapeDtypeStruct((b, s, KV_WIDTH), BF16),
                   jax.ShapeDtypeStruct((b, KV_WIDTH, s), BF16)],
        compiler_params=pltpu.CompilerParams(dimension_semantics=("parallel", "parallel"),
                                             vmem_limit_bytes=VMEM_LIMIT),
        name="proj",
    )(x, shift, scale, gain, w_in, q_gain_t, k_gain, bd, cos, sin, cos_t, sin_t)


def _fourier_kernel(u_ref, ma_ref, mb_ref, mc_ref, wf_ref, o_ref, zr_sc, zi_sc, tsc, *, unroll):
    n2_count = DFT_INNER
    n1_count = u_ref.shape[0] // n2_count
    c = FOURIER_GROUP_DIM
    t_rows = 2 * n1_count
    t_pitch = t_rows + SUBLANES
    o_pitch = n1_count + SUBLANES

    fw = _dot(mc_ref[...], wf_ref[...]).astype(BF16)
    fw_cat = jnp.concatenate([fw[:c], fw[c:]], axis=1)

    def fill(i, carry):
        src = pl.multiple_of(i * n2_count, n2_count)
        dst = pl.multiple_of(i * ROW_PITCH, SUBLANES)
        z = _dot(u_ref[pl.ds(src, n2_count), :], fw_cat)
        zr_sc[pl.ds(dst, n2_count), :] = z[:, :c]
        zi_sc[pl.ds(dst, n2_count), :] = z[:, c:]
        return carry
    lax.fori_loop(0, n1_count, fill, 0, unroll=min(unroll, n1_count))

    def stage_a(n2, carry):
        zr = zr_sc[pl.ds(n2, n1_count, stride=ROW_PITCH), :]
        zi = zi_sc[pl.ds(n2, n1_count, stride=ROW_PITCH), :]
        z2 = jnp.concatenate([zr, zi], axis=0).astype(BF16)
        dst = pl.multiple_of(n2 * t_pitch, SUBLANES)
        tsc[pl.ds(dst, t_rows), :] = _dot(ma_ref[n2], z2)
        return carry
    lax.fori_loop(0, n2_count, stage_a, 0, unroll=min(2 * unroll, n2_count))

    def stage_b(k1, carry):
        tr = tsc[pl.ds(k1, n2_count, stride=t_pitch), :]
        ti = tsc[pl.ds(n1_count + k1, n2_count, stride=t_pitch), :]
        t2 = jnp.concatenate([tr, ti], axis=0).astype(BF16)
        o_ref[pl.ds(k1, n2_count, stride=o_pitch), :] = _dot(mb_ref[...], t2)
        return carry
    lax.fori_loop(0, n1_count, stage_b, 0, unroll=min(unroll, n1_count))
    for pad_row in range(SUBLANES):
        o_ref[pl.ds(n1_count + pad_row, n2_count, stride=o_pitch), :] = jnp.zeros((n2_count, c), F32)


def _fourier(u, ma, mb, mc, w_four, unroll):
    b, s, _ = u.shape
    c = FOURIER_GROUP_DIM
    n1 = s // DFT_INNER
    single = pl.Buffered(1)
    return pl.pallas_call(
        functools.partial(_fourier_kernel, unroll=unroll),
        grid=(b, N_FOURIER_GROUPS),
        in_specs=[pl.BlockSpec((None, s, c), lambda bi, g: (bi, 0, g)),
                  pl.BlockSpec(ma.shape, lambda bi, g: (0, 0, 0), pipeline_mode=single),
                  pl.BlockSpec(mb.shape, lambda bi, g: (0, 0), pipeline_mode=single),
                  pl.BlockSpec(mc.shape, lambda bi, g: (0, 0), pipeline_mode=single),
                  pl.BlockSpec((None, c, c), lambda bi, g: (g, 0, 0))],
        out_specs=pl.BlockSpec((None, DFT_INNER * (n1 + SUBLANES), c), lambda bi, g: (bi, 0, g)),
        out_shape=jax.ShapeDtypeStruct((b, DFT_INNER * (n1 + SUBLANES), FOURIER_WIDTH), F32),
        scratch_shapes=[pltpu.VMEM((n1 * ROW_PITCH, c), F32),
                        pltpu.VMEM((n1 * ROW_PITCH, c), F32),
                        pltpu.VMEM((DFT_INNER * (2 * n1 + SUBLANES), c), F32)],
        compiler_params=pltpu.CompilerParams(dimension_semantics=("parallel", "parallel"),
                                             vmem_limit_bytes=VMEM_LIMIT),
        name="fourier",
    )(u, ma, mb, mc, w_four)


def _fourier_tables(s):
    n2c = DFT_INNER
    n1c = s // n2c
    k1 = np.arange(n1c, dtype=np.float64)
    n = (n2c * np.arange(n1c)[None, :] + np.arange(n2c)[:, None]).astype(np.float64)
    ang = 2.0 * np.pi * k1[None, :, None] * n[:, None, :] / s
    ca, sa = np.cos(ang), np.sin(ang)
    ma = np.concatenate([np.concatenate([ca, sa], axis=2),
                         np.concatenate([-sa, ca], axis=2)], axis=1)
    kk = np.arange(n2c, dtype=np.float64)
    a2 = 2.0 * np.pi * np.outer(kk, kk) / n2c
    mb = np.concatenate([np.cos(a2), np.sin(a2)], axis=1)
    cc = np.arange(FOURIER_GROUP_DIM, dtype=np.float64)
    ac = 2.0 * np.pi * np.outer(cc, cc) / FOURIER_GROUP_DIM
    norm = 1.0 / math.sqrt(s * FOURIER_GROUP_DIM)
    mc = np.concatenate([np.cos(ac), -np.sin(ac)], axis=0) * norm
    return (ma.astype(np.float32), mb.astype(np.float32), mc.astype(np.float32))


def _attn_kernel(bound_ref, qt_ref, k_ref, vt_ref, o_ref, acc_ref, p_ref, l_ref, *, tk, unroll):
    tq = qt_ref.shape[1]
    n_keys = k_ref.shape[0]
    g = pl.program_id(1)
    parts = []
    for h in range(Q_PER_KV):
        qh = qt_ref[h * HEAD_DIM:(h + 1) * HEAD_DIM, :]
        z = jnp.zeros_like(qh)
        parts.append(jnp.where(g == 0, jnp.concatenate([qh, z], axis=0),
                               jnp.concatenate([z, qh], axis=0)))
    w = jnp.concatenate(parts, axis=1)
    rq = w.shape[1]
    acc_ref[...] = jnp.zeros(acc_ref.shape, F32)

    unshifted_ok = bound_ref[0] <= MAX_UNSHIFTED_SCORE

    def blocks(j):
        start = pl.multiple_of(j * tk, tk)
        return k_ref[pl.ds(start, tk), :], vt_ref[:, pl.ds(start, tk)]

    def sublane_sums(p):
        return p.reshape(p.shape[0] // SUBLANES, SUBLANES, p.shape[1]).sum(axis=0)

    @pl.when(unshifted_ok)
    def _():
        kb0, _ = blocks(0)
        p0 = jnp.exp2(_dot(kb0, w))
        l_ref[...] = sublane_sums(p0)
        p_ref[...] = p0.astype(BF16)

        def body(j, carry):
            kb, _ = blocks(j)
            _, vb_prev = blocks(j - 1)
            for c0 in range(0, rq, MXU_COLS):
                cols = slice(c0, c0 + MXU_COLS)
                s = _dot(kb, w[:, cols])
                acc_ref[:, cols] += _dot(vb_prev, p_ref[:, cols])
                p = jnp.exp2(s)
                l_ref[:, cols] += sublane_sums(p)
                p_ref[:, cols] = p.astype(BF16)
            return carry
        n_blocks = n_keys // tk
        lax.fori_loop(1, n_blocks, body, 0, unroll=unroll)
        _, vb_last = blocks(n_blocks - 1)
        acc_ref[...] += _dot(vb_last, p_ref[...])

    @pl.when(jnp.logical_not(unshifted_ok))
    def _():
        l_ref[...] = jnp.zeros(l_ref.shape, F32)

        def body(j, m_prev):
            kb, vb = blocks(j)
            s = _dot(kb, w)
            m_next = jnp.maximum(m_prev, jnp.max(s, axis=0, keepdims=True))
            alpha = jnp.exp2(m_prev - m_next)
            p = jnp.exp2(s - m_next)
            l_ref[...] = alpha * l_ref[...] + sublane_sums(p)
            acc_ref[...] = alpha * acc_ref[...] + _dot(vb, p.astype(BF16))
            return m_next
        lax.fori_loop(0, n_keys // tk, body, jnp.full((1, rq), -jnp.inf, F32))

    out_t = acc_ref[...] / jnp.sum(l_ref[...], axis=0, keepdims=True)
    out_t = jnp.concatenate([out_t[:, h * tq:(h + 1) * tq] for h in range(Q_PER_KV)], axis=0)
    o_ref[...] = out_t.T.astype(o_ref.dtype)


def _attention(qt, k, vt, q_gain, k_gain, tq, tk, unroll):
    b, _, s = qt.shape
    n_keys = k.shape[1]
    gw = Q_PER_KV * HEAD_DIM
    bound = (jnp.max(jnp.abs(q_gain)) * jnp.max(jnp.abs(k_gain))
             * (HEAD_DIM * Q_SCALE * SCORE_BOUND_SLACK)).reshape(1).astype(F32)
    return pl.pallas_call(
        functools.partial(_attn_kernel, tk=tk, unroll=unroll),
        grid=(b, N_KV_HEADS, s // tq),
        in_specs=[pl.BlockSpec(memory_space=pltpu.SMEM),
                  pl.BlockSpec((None, gw, tq), lambda bi, g, i: (bi, g, i)),
                  pl.BlockSpec((None, n_keys, KV_WIDTH), lambda bi, g, i: (bi, 0, 0)),
                  pl.BlockSpec((None, HEAD_DIM, n_keys), lambda bi, g, i: (bi, g, 0))],
        out_specs=pl.BlockSpec((None, tq, gw), lambda bi, g, i: (bi, i, g)),
        out_shape=jax.ShapeDtypeStruct((b, s, ATTN_WIDTH), BF16),
        scratch_shapes=[pltpu.VMEM((HEAD_DIM, Q_PER_KV * tq), F32),
                        pltpu.VMEM((tk, Q_PER_KV * tq), BF16),
                        pltpu.VMEM((SUBLANES, Q_PER_KV * tq), F32)],
        compiler_params=pltpu.CompilerParams(
            dimension_semantics=("parallel", "parallel", "parallel"),
            vmem_limit_bytes=VMEM_LIMIT),
        name="attention",
    )(bound, qt, k, vt)


def _ffn_kernel(x_ref, fm_ref, at_ref, gt1_ref, sh2_ref, sc2_ref, gt2_ref, gffn_ref, gfin_ref,
                wo_ref, wg_ref, wu_ref, wd_ref, o_ref, *, ff_chunk, row_groups):
    tm = x_ref.shape[0]
    gm = tm // row_groups
    rows = [slice(i * gm, (i + 1) * gm) for i in range(row_groups)]
    n1 = fm_ref.shape[1] - SUBLANES
    k2g = gm // n1
    fms = [fm_ref[i * k2g:(i + 1) * k2g, :n1, :].reshape(gm, FOURIER_WIDTH) for i in range(row_groups)]
    mix = [_dot(f.astype(BF16), wo_ref[:FOURIER_WIDTH, :])
           + _dot(at_ref[r, :], wo_ref[FOURIER_WIDTH:, :]) for f, r in zip(fms, rows)]
    x1 = [x_ref[r, :] + gt1_ref[...] * m for r, m in zip(rows, mix)]
    h2 = [(_rms(x) * gffn_ref[...] * (1.0 + sc2_ref[...]) + sh2_ref[...]).astype(BF16) for x in x1]
    d_ff = wg_ref.shape[1]
    ffn = [None] * row_groups
    for c0 in range(0, d_ff, ff_chunk):
        gate = [_dot(h, wg_ref[:, c0:c0 + ff_chunk]) for h in h2]
        up = [_dot(h, wu_ref[:, c0:c0 + ff_chunk]) for h in h2]
        act = [(g / (1.0 + jnp.exp(-g)) * u).astype(BF16) for g, u in zip(gate, up)]
        for i, a in enumerate(act):
            part = _dot(a, wd_ref[c0:c0 + ff_chunk, :])
            ffn[i] = part if ffn[i] is None else ffn[i] + part
    for r, x, f in zip(rows, x1, ffn):
        o_ref[r, :] = _rms(x + gt2_ref[...] * f) * gfin_ref[...]


def _ffn(x, fm, at, gt1, sh2, sc2, gt2, g_ffn, g_final, w_out, w_gate, w_up, w_down, tm, ff_chunk):
    b, s, d = x.shape
    d_ff = w_gate.shape[1]
    n1 = s // DFT_INNER
    fm = fm.reshape(b, DFT_INNER, n1 + SUBLANES, FOURIER_WIDTH)
    vec = pl.BlockSpec((None, 1, d), lambda bi, i: (bi, 0, 0))
    single = pl.Buffered(1)
    const = lambda shape: pl.BlockSpec(shape, lambda bi, i: (0,) * len(shape))
    weight = lambda shape: pl.BlockSpec(shape, lambda bi, i: (0,) * len(shape), pipeline_mode=single)
    return pl.pallas_call(
        functools.partial(_ffn_kernel, ff_chunk=ff_chunk, row_groups=2),
        grid=(b, s // tm),
        in_specs=[pl.BlockSpec((None, tm, d), lambda bi, i: (bi, i, 0)),
                  pl.BlockSpec((None, tm // n1, n1 + SUBLANES, FOURIER_WIDTH),
                               lambda bi, i: (bi, i, 0, 0)),
                  pl.BlockSpec((None, tm, ATTN_WIDTH), lambda bi, i: (bi, i, 0)),
                  vec, vec, vec, vec, const((1, d)), const((1, d)),
                  weight((d, d)), weight((d, d_ff)), weight((d, d_ff)), weight((d_ff, d))],
        out_specs=pl.BlockSpec((None, tm, d), lambda bi, i: (bi, i, 0)),
        out_shape=jax.ShapeDtypeStruct((b, s, d), F32),
        compiler_params=pltpu.CompilerParams(dimension_semantics=("parallel", "parallel"),
                                             vmem_limit_bytes=VMEM_LIMIT),
        name="ffn",
    )(x, fm, at, gt1, sh2, sc2, gt2, g_ffn, g_final, w_out, w_gate, w_up, w_down)


def _rope_tables(n_tokens):
    n_freq = HEAD_DIM // 4
    inv_freq = ROPE_THETA ** (-np.arange(n_freq, dtype=np.float64) / n_freq)
    t = np.arange(n_tokens)
    row_ang = (t // GRID_W).astype(np.float64)[:, None] * inv_freq[None, :]
    col_ang = (t % GRID_W).astype(np.float64)[:, None] * inv_freq[None, :]
    ang = np.concatenate([row_ang, row_ang, col_ang, col_ang], axis=1)
    sign = np.tile(np.concatenate([-np.ones(n_freq), np.ones(n_freq)]), 2)
    cos = np.cos(ang)
    sin = np.sin(ang) * sign[None, :]
    reps = LANES // HEAD_DIM
    f32 = lambda a: np.ascontiguousarray(a).astype(np.float32)
    return f32(np.tile(cos, (1, reps))), f32(np.tile(sin, (1, reps))), f32(cos.T), f32(sin.T)


def _head_block_diag(width):
    idx = np.arange(width) // HEAD_DIM
    return (idx[:, None] == idx[None, :]).astype(np.float32)


def _key_block(n_keys):
    for tk in range(256, 0, -LANES):
        if n_keys % tk == 0:
            return tk
    raise ValueError(f"n_keys={n_keys} is not a multiple of {LANES}")


def kernel(x, c, ctx, c_ctx, w_ada, b_ada, g_mix, w_in, w_four, q_gain, k_gain, w_out, g_ffn,
           w_gate, w_up, w_down, g_final):
    b, s, d = x.shape
    n_ctx = ctx.shape[1]
    assert w_ada.shape[0] == 1, "single-layer block"
    assert s % (DFT_INNER * SUBLANES) == 0 and n_ctx % LANES == 0

    pad = (-(b + 1)) % SUBLANES
    cond = jnp.concatenate([c, c_ctx[None, :], jnp.zeros((pad, d), F32)], axis=0)
    ada = _adaln(cond, w_ada[0], b_ada[0][None, :])
    sh1, sc1, gt1, sh2, sc2, gt2 = [ada[:, i * d:(i + 1) * d] for i in range(6)]
    per_batch = lambda m: m[:b, None, :]
    for_ctx = lambda m: jnp.broadcast_to(m[b][None, None, :], (b, 1, d))

    w_in_b = w_in[0].astype(BF16)
    bd = jnp.asarray(_head_block_diag(KV_WIDTH)).astype(BF16)
    qg_t = jnp.broadcast_to((q_gain[0] * Q_SCALE)[:, None], (HEAD_DIM, LANES))
    kg = jnp.tile(k_gain[0], N_KV_HEADS)[None, :]
    g_mix2 = g_mix[0][None, :]

    cos, sin, cos_t, sin_t = [jnp.asarray(t) for t in _rope_tables(s)]
    tm_proj = min(512, s)
    u, qt, k, vt = _project(x, per_batch(sh1), per_batch(sc1), g_mix2, w_in_b, qg_t, kg, bd,
                            cos, sin, cos_t, sin_t, tm_proj)
    _, _, kc, vtc = _project(ctx, for_ctx(sh1), for_ctx(sc1), g_mix2, w_in_b, qg_t, kg, bd,
                             jnp.ones((n_ctx, LANES), F32), jnp.zeros((n_ctx, LANES), F32),
                             jnp.ones((HEAD_DIM, n_ctx), F32), jnp.zeros((HEAD_DIM, n_ctx), F32), n_ctx)
    n_keys = s + n_ctx
    k_all = jnp.concatenate([k, kc], axis=1)
    vt_all = jnp.concatenate([vt, vtc], axis=2)

    ma, mb, mc = _fourier_tables(s)
    fm = _fourier(u, jnp.asarray(ma).astype(BF16), jnp.asarray(mb).astype(BF16),
                  jnp.asarray(mc), w_four[0], unroll=8)

    at = _attention(qt, k_all, vt_all, q_gain, k_gain, tq=min(512, s), tk=_key_block(n_keys), unroll=True)

    tm_ffn = min(512, s)
    d_ff = w_gate.shape[2]
    ff_chunk = d_ff
    return _ffn(x, fm, at, per_batch(gt1), per_batch(sh2), per_batch(sc2), per_batch(gt2),
                g_ffn[0][None, :], g_final[None, :], w_out[0].astype(BF16), w_gate[0].astype(BF16),
                w_up[0].astype(BF16), w_down[0].astype(BF16), tm_ffn, ff_chunk)
```

```python
import functools
import math

import numpy as np
import jax
import jax.numpy as jnp
from jax import lax
from jax.experimental import pallas as pl
from jax.experimental.pallas import tpu as pltpu

F32 = jnp.float32
BF16 = jnp.bfloat16

EPS = 1e-6
ROPE_THETA = 10000.0
GRID_W = 64
HEAD_DIM = 64
N_HEADS = 8
N_KV_HEADS = 2
Q_PER_KV = N_HEADS // N_KV_HEADS
N_FOURIER_GROUPS = 4
FOURIER_GROUP_DIM = 128
FOURIER_WIDTH = N_FOURIER_GROUPS * FOURIER_GROUP_DIM
ATTN_WIDTH = N_HEADS * HEAD_DIM
KV_WIDTH = N_KV_HEADS * HEAD_DIM

LANES = 128
MXU_COLS = 256
SUBLANES = 8
DFT_INNER = 128
ROW_PITCH = DFT_INNER + SUBLANES
VMEM_LIMIT = 56 * 1024 * 1024

Q_SCALE = (HEAD_DIM ** -0.5) * math.log2(math.e)
SCORE_BOUND_SLACK = 1.02
MAX_UNSHIFTED_SCORE = 64.0


def _dot(a, b):
    return jnp.dot(a, b, preferred_element_type=F32)


def _split_bf16(a):
    hi = a.astype(BF16)
    return hi, (a - hi.astype(F32)).astype(BF16)


def _rms(x):
    return x * lax.rsqrt(jnp.mean(x * x, axis=-1, keepdims=True) + EPS)


def _adaln_kernel(cond_ref, w_ref, b_ref, o_ref):
    c = cond_ref[...]
    s_hi, s_lo = _split_bf16(c / (1.0 + jnp.exp(-c)))
    w_hi, w_lo = _split_bf16(w_ref[...])
    o_ref[...] = _dot(s_hi, w_hi) + _dot(s_lo, w_hi) + _dot(s_hi, w_lo) + b_ref[...]


def _adaln(cond, w, b):
    rows, d = cond.shape
    n = w.shape[1]
    tn = n // 4
    return pl.pallas_call(
        _adaln_kernel,
        grid=(n // tn,),
        in_specs=[pl.BlockSpec((rows, d), lambda j: (0, 0)),
                  pl.BlockSpec((d, tn), lambda j: (0, j)),
                  pl.BlockSpec((1, tn), lambda j: (0, j))],
        out_specs=pl.BlockSpec((rows, tn), lambda j: (0, j)),
        out_shape=jax.ShapeDtypeStruct((rows, n), F32),
        compiler_params=pltpu.CompilerParams(dimension_semantics=("arbitrary",),
                                             vmem_limit_bytes=VMEM_LIMIT),
        name="adaln",
    )(cond, w, b)


def _proj_kernel(x_ref, sh_ref, sc_ref, g_ref, w_ref, qgt_ref, kg_ref, bd_ref, cos_ref, sin_ref,
                 cost_ref, sint_ref, u_ref, qt_ref, k_ref, vt_ref, *, row_groups):
    tm = x_ref.shape[0]
    gm = tm // row_groups
    rows = [slice(i * gm, (i + 1) * gm) for i in range(row_groups)]
    o1 = FOURIER_WIDTH
    o2 = o1 + ATTN_WIDTH
    o3 = o2 + KV_WIDTH
    hs = [((_rms(x_ref[r, :]) * g_ref[...]) * (1.0 + sc_ref[...]) + sh_ref[...]).astype(BF16)
          for r in rows]
    ps = [_dot(h, w_ref[...]) for h in hs]

    lane = lax.broadcasted_iota(jnp.int32, (gm, LANES), 1)
    first_half = (lane & 31) < 16
    gain_t = jnp.tile(qgt_ref[...], (1, gm // LANES))
    quarter = HEAD_DIM // 4

    for r, p in zip(rows, ps):
        u_ref[r, :] = p[:, :o1].astype(u_ref.dtype)
        vt_ref[:, r] = p[:, o3:].T.astype(vt_ref.dtype)

        kx = p[:, o2:o3]
        hi, lo = _split_bf16(kx * kx)
        ss = _dot(hi, bd_ref[...]) + _dot(lo, bd_ref[...])
        kn = kx * lax.rsqrt(ss * (1.0 / HEAD_DIM) + EPS) * kg_ref[...]
        partner = jnp.where(first_half, pltpu.roll(kn, LANES - quarter, axis=1),
                            pltpu.roll(kn, quarter, axis=1))
        k_ref[r, :] = (kn * cos_ref[r, :] + partner * sin_ref[r, :]).astype(k_ref.dtype)

        cos_t = cost_ref[:, r]
        sin_t = sint_ref[:, r]
        for j in range(ATTN_WIDTH // LANES):
            qtj = p[:, o1 + j * LANES:o1 + (j + 1) * LANES].T
            for hh in range(LANES // HEAD_DIM):
                qh = qtj[hh * HEAD_DIM:(hh + 1) * HEAD_DIM]
                ss = jnp.sum(qh * qh, axis=0, keepdims=True)
                qn = qh * lax.rsqrt(ss * (1.0 / HEAD_DIM) + EPS) * gain_t
                partner = jnp.concatenate([qn[quarter:2 * quarter], qn[:quarter],
                                           qn[3 * quarter:], qn[2 * quarter:3 * quarter]], axis=0)
                row0 = j * LANES + hh * HEAD_DIM
                qt_ref[row0:row0 + HEAD_DIM, r] = (qn * cos_t + partner * sin_t).astype(qt_ref.dtype)


def _project(x, shift, scale, gain, w_in, q_gain_t, k_gain, bd, cos, sin, cos_t, sin_t, tm):
    b, s, d = x.shape
    in_w = w_in.shape[1]
    vec = pl.BlockSpec((None, 1, d), lambda bi, i: (bi, 0, 0))
    const = lambda shape: pl.BlockSpec(shape, lambda bi, i: (0,) * len(shape))
    row_groups = max(1, tm // MXU_COLS)
    return pl.pallas_call(
        functools.partial(_proj_kernel, row_groups=row_groups),
        grid=(b, s // tm),
        in_specs=[pl.BlockSpec((None, tm, d), lambda bi, i: (bi, i, 0)),
                  vec, vec, const((1, d)), const((d, in_w)),
                  const((HEAD_DIM, LANES)), const((1, KV_WIDTH)), const((KV_WIDTH, KV_WIDTH)),
                  pl.BlockSpec((tm, LANES), lambda bi, i: (i, 0)),
                  pl.BlockSpec((tm, LANES), lambda bi, i: (i, 0)),
                  pl.BlockSpec((HEAD_DIM, tm), lambda bi, i: (0, i)),
                  pl.BlockSpec((HEAD_DIM, tm), lambda bi, i: (0, i))],
        out_specs=[pl.BlockSpec((None, tm, FOURIER_WIDTH), lambda bi, i: (bi, i, 0)),
                   pl.BlockSpec((None, ATTN_WIDTH, tm), lambda bi, i: (bi, 0, i)),
                   pl.BlockSpec((None, tm, KV_WIDTH), lambda bi, i: (bi, i, 0)),
                   pl.BlockSpec((None, KV_WIDTH, tm), lambda bi, i: (bi, 0, i))],
        out_shape=[jax.ShapeDtypeStruct((b, s, FOURIER_WIDTH), BF16),
                   jax.ShapeDtypeStruct((b, ATTN_WIDTH, s), BF16),
                   jax.ShapeDtypeStruct((b, s, KV_WIDTH), BF16),
                   jax.ShapeDtypeStruct((b, KV_WIDTH, s), BF16)],
        compiler_params=pltpu.CompilerParams(dimension_semantics=("parallel", "parallel"),
                                             vmem_limit_bytes=VMEM_LIMIT),
        name="proj",
    )(x, shift, scale, gain, w_in, q_gain_t, k_gain, bd, cos, sin, cos_t, sin_t)


def _fourier_kernel(u_ref, ma_ref, mb_ref, mc_ref, wf_ref, o_ref, zr_sc, zi_sc, tsc, *, unroll):
    n2_count = DFT_INNER
    n1_count = u_ref.shape[0] // n2_count
    c = FOURIER_GROUP_DIM
    t_rows = 2 * n1_count
    t_pitch = t_rows + SUBLANES
    o_pitch = n1_count + SUBLANES

    fw = _dot(mc_ref[...], wf_ref[...]).astype(BF16)
    fw_cat = jnp.concatenate([fw[:c], fw[c:]], axis=1)

    def fill(i, carry):
        src = pl.multiple_of(i * n2_count, n2_count)
        dst = pl.multiple_of(i * ROW_PITCH, SUBLANES)
        z = _dot(u_ref[pl.ds(src, n2_count), :], fw_cat)
        zr_sc[pl.ds(dst, n2_count), :] = z[:, :c]
        zi_sc[pl.ds(dst, n2_count), :] = z[:, c:]
        return carry
    lax.fori_loop(0, n1_count, fill, 0, unroll=min(unroll, n1_count))

    def stage_a(n2, carry):
        zr = zr_sc[pl.ds(n2, n1_count, stride=ROW_PITCH), :]
        zi = zi_sc[pl.ds(n2, n1_count, stride=ROW_PITCH), :]
        z2 = jnp.concatenate([zr, zi], axis=0).astype(BF16)
        dst = pl.multiple_of(n2 * t_pitch, SUBLANES)
        tsc[pl.ds(dst, t_rows), :] = _dot(ma_ref[n2], z2)
        return carry
    lax.fori_loop(0, n2_count, stage_a, 0, unroll=min(2 * unroll, n2_count))

    def stage_b(pair, carry):
        t2 = []
        for k1 in (2 * pair, 2 * pair + 1):
            tr = tsc[pl.ds(k1, n2_count, stride=t_pitch), :]
            ti = tsc[pl.ds(n1_count + k1, n2_count, stride=t_pitch), :]
            t2.append(jnp.concatenate([tr, ti], axis=0).astype(BF16))
        y = _dot(mb_ref[...], jnp.concatenate(t2, axis=1))
        o_ref[pl.ds(2 * pair, n2_count, stride=o_pitch), :] = y[:, :c]
        o_ref[pl.ds(2 * pair + 1, n2_count, stride=o_pitch), :] = y[:, c:]
        return carry
    lax.fori_loop(0, n1_count // 2, stage_b, 0, unroll=min(unroll, n1_count // 2))
    for pad_row in range(SUBLANES):
        o_ref[pl.ds(n1_count + pad_row, n2_count, stride=o_pitch), :] = jnp.zeros((n2_count, c), F32)


def _fourier(u, ma, mb, mc, w_four, unroll):
    b, s, _ = u.shape
    c = FOURIER_GROUP_DIM
    n1 = s // DFT_INNER
    single = pl.Buffered(1)
    return pl.pallas_call(
        functools.partial(_fourier_kernel, unroll=unroll),
        grid=(b, N_FOURIER_GROUPS),
        in_specs=[pl.BlockSpec((None, s, c), lambda bi, g: (bi, 0, g)),
                  pl.BlockSpec(ma.shape, lambda bi, g: (0, 0, 0), pipeline_mode=single),
                  pl.BlockSpec(mb.shape, lambda bi, g: (0, 0), pipeline_mode=single),
                  pl.BlockSpec(mc.shape, lambda bi, g: (0, 0), pipeline_mode=single),
                  pl.BlockSpec((None, c, c), lambda bi, g: (g, 0, 0))],
        out_specs=pl.BlockSpec((None, DFT_INNER * (n1 + SUBLANES), c), lambda bi, g: (bi, 0, g)),
        out_shape=jax.ShapeDtypeStruct((b, DFT_INNER * (n1 + SUBLANES), FOURIER_WIDTH), F32),
        scratch_shapes=[pltpu.VMEM((n1 * ROW_PITCH, c), F32),
                        pltpu.VMEM((n1 * ROW_PITCH, c), F32),
                        pltpu.VMEM((DFT_INNER * (2 * n1 + SUBLANES), c), F32)],
        compiler_params=pltpu.CompilerParams(dimension_semantics=("parallel", "parallel"),
                                             vmem_limit_bytes=VMEM_LIMIT),
        name="fourier",
    )(u, ma, mb, mc, w_four)


def _fourier_tables(s):
    n2c = DFT_INNER
    n1c = s // n2c
    k1 = np.arange(n1c, dtype=np.float64)
    n = (n2c * np.arange(n1c)[None, :] + np.arange(n2c)[:, None]).astype(np.float64)
    ang = 2.0 * np.pi * k1[None, :, None] * n[:, None, :] / s
    ca, sa = np.cos(ang), np.sin(ang)
    ma = np.concatenate([np.concatenate([ca, sa], axis=2),
                         np.concatenate([-sa, ca], axis=2)], axis=1)
    kk = np.arange(n2c, dtype=np.float64)
    a2 = 2.0 * np.pi * np.outer(kk, kk) / n2c
    mb = np.concatenate([np.cos(a2), np.sin(a2)], axis=1)
    cc = np.arange(FOURIER_GROUP_DIM, dtype=np.float64)
    ac = 2.0 * np.pi * np.outer(cc, cc) / FOURIER_GROUP_DIM
    norm = 1.0 / math.sqrt(s * FOURIER_GROUP_DIM)
    mc = np.concatenate([np.cos(ac), -np.sin(ac)], axis=0) * norm
    return (ma.astype(np.float32), mb.astype(np.float32), mc.astype(np.float32))


def _attn_kernel(bound_ref, qt_ref, k_ref, vt_ref, o_ref, acc_ref, p_ref, l_ref, *, tk, unroll):
    tq = qt_ref.shape[1]
    n_keys = k_ref.shape[0]
    g = pl.program_id(1)
    parts = []
    for h in range(Q_PER_KV):
        qh = qt_ref[h * HEAD_DIM:(h + 1) * HEAD_DIM, :]
        z = jnp.zeros_like(qh)
        parts.append(jnp.where(g == 0, jnp.concatenate([qh, z], axis=0),
                               jnp.concatenate([z, qh], axis=0)))
    w = jnp.concatenate(parts, axis=1)
    rq = w.shape[1]
    acc_ref[...] = jnp.zeros(acc_ref.shape, F32)

    unshifted_ok = bound_ref[0] <= MAX_UNSHIFTED_SCORE

    def blocks(j):
        start = pl.multiple_of(j * tk, tk)
        return k_ref[pl.ds(start, tk), :], vt_ref[:, pl.ds(start, tk)]

    def sublane_sums(p):
        return p.reshape(p.shape[0] // SUBLANES, SUBLANES, p.shape[1]).sum(axis=0)

    @pl.when(unshifted_ok)
    def _():
        kb0, _ = blocks(0)
        p0 = jnp.exp2(_dot(kb0, w))
        l_ref[...] = sublane_sums(p0)
        p_ref[...] = p0.astype(BF16)

        def body(j, carry):
            kb, _ = blocks(j)
            _, vb_prev = blocks(j - 1)
            for c0 in range(0, rq, MXU_COLS):
                cols = slice(c0, c0 + MXU_COLS)
                s = _dot(kb, w[:, cols])
                acc_ref[:, cols] += _dot(vb_prev, p_ref[:, cols])
                p = jnp.exp2(s)
                l_ref[:, cols] += sublane_sums(p)
                p_ref[:, cols] = p.astype(BF16)
            return carry
        n_blocks = n_keys // tk
        lax.fori_loop(1, n_blocks, body, 0, unroll=unroll)
        _, vb_last = blocks(n_blocks - 1)
        acc_ref[...] += _dot(vb_last, p_ref[...])

    @pl.when(jnp.logical_not(unshifted_ok))
    def _():
        l_ref[...] = jnp.zeros(l_ref.shape, F32)

        def body(j, m_prev):
            kb, vb = blocks(j)
            s = _dot(kb, w)
            m_next = jnp.maximum(m_prev, jnp.max(s, axis=0, keepdims=True))
            alpha = jnp.exp2(m_prev - m_next)
            p = jnp.exp2(s - m_next)
            l_ref[...] = alpha * l_ref[...] + sublane_sums(p)
            acc_ref[...] = alpha * acc_ref[...] + _dot(vb, p.astype(BF16))
            return m_next
        lax.fori_loop(0, n_keys // tk, body, jnp.full((1, rq), -jnp.inf, F32))

    out_t = acc_ref[...] / jnp.sum(l_ref[...], axis=0, keepdims=True)
    out_t = jnp.concatenate([out_t[:, h * tq:(h + 1) * tq] for h in range(Q_PER_KV)], axis=0)
    o_ref[...] = out_t.T.astype(o_ref.dtype)


def _attention(qt, k, vt, q_gain, k_gain, tq, tk, unroll):
    b, _, s = qt.shape
    n_keys = k.shape[1]
    gw = Q_PER_KV * HEAD_DIM
    bound = (jnp.max(jnp.abs(q_gain)) * jnp.max(jnp.abs(k_gain))
             * (HEAD_DIM * Q_SCALE * SCORE_BOUND_SLACK)).reshape(1).astype(F32)
    return pl.pallas_call(
        functools.partial(_attn_kernel, tk=tk, unroll=unroll),
        grid=(b, N_KV_HEADS, s // tq),
        in_specs=[pl.BlockSpec(memory_space=pltpu.SMEM),
                  pl.BlockSpec((None, gw, tq), lambda bi, g, i: (bi, g, i)),
                  pl.BlockSpec((None, n_keys, KV_WIDTH), lambda bi, g, i: (bi, 0, 0)),
                  pl.BlockSpec((None, HEAD_DIM, n_keys), lambda bi, g, i: (bi, g, 0))],
        out_specs=pl.BlockSpec((None, tq, gw), lambda bi, g, i: (bi, i, g)),
        out_shape=jax.ShapeDtypeStruct((b, s, ATTN_WIDTH), BF16),
        scratch_shapes=[pltpu.VMEM((HEAD_DIM, Q_PER_KV * tq), F32),
                        pltpu.VMEM((tk, Q_PER_KV * tq), BF16),
                        pltpu.VMEM((SUBLANES, Q_PER_KV * tq), F32)],
        compiler_params=pltpu.CompilerParams(
            dimension_semantics=("parallel", "parallel", "parallel"),
            vmem_limit_bytes=VMEM_LIMIT),
        name="attention",
    )(bound, qt, k, vt)


def _ffn_kernel(x_ref, fm_ref, at_ref, gt1_ref, sh2_ref, sc2_ref, gt2_ref, gffn_ref, gfin_ref,
                wo_ref, wg_ref, wu_ref, wd_ref, o_ref, *, ff_chunk, row_groups):
    tm = x_ref.shape[0]
    gm = tm // row_groups
    rows = [slice(i * gm, (i + 1) * gm) for i in range(row_groups)]
    n1 = fm_ref.shape[1] - SUBLANES
    k2g = gm // n1
    fms = [fm_ref[i * k2g:(i + 1) * k2g, :n1, :].reshape(gm, FOURIER_WIDTH) for i in range(row_groups)]
    mix = [_dot(f.astype(BF16), wo_ref[:FOURIER_WIDTH, :])
           + _dot(at_ref[r, :], wo_ref[FOURIER_WIDTH:, :]) for f, r in zip(fms, rows)]
    x1 = [x_ref[r, :] + gt1_ref[...] * m for r, m in zip(rows, mix)]
    h2 = [(_rms(x) * gffn_ref[...] * (1.0 + sc2_ref[...]) + sh2_ref[...]).astype(BF16) for x in x1]
    d_ff = wg_ref.shape[1]
    ffn = [None] * row_groups
    for c0 in range(0, d_ff, ff_chunk):
        gate = [_dot(h, wg_ref[:, c0:c0 + ff_chunk]) for h in h2]
        up = [_dot(h, wu_ref[:, c0:c0 + ff_chunk]) for h in h2]
        act = [(g / (1.0 + jnp.exp(-g)) * u).astype(BF16) for g, u in zip(gate, up)]
        for i, a in enumerate(act):
            part = _dot(a, wd_ref[c0:c0 + ff_chunk, :])
            ffn[i] = part if ffn[i] is None else ffn[i] + part
    for r, x, f in zip(rows, x1, ffn):
        o_ref[r, :] = _rms(x + gt2_ref[...] * f) * gfin_ref[...]


def _ffn(x, fm, at, gt1, sh2, sc2, gt2, g_ffn, g_final, w_out, w_gate, w_up, w_down, tm, ff_chunk):
    b, s, d = x.shape
    d_ff = w_gate.shape[1]
    n1 = s // DFT_INNER
    fm = fm.reshape(b, DFT_INNER, n1 + SUBLANES, FOURIER_WIDTH)
    vec = pl.BlockSpec((None, 1, d), lambda bi, i: (bi, 0, 0))
    single = pl.Buffered(1)
    const = lambda shape: pl.BlockSpec(shape, lambda bi, i: (0,) * len(shape))
    weight = lambda shape: pl.BlockSpec(shape, lambda bi, i: (0,) * len(shape), pipeline_mode=single)
    return pl.pallas_call(
        functools.partial(_ffn_kernel, ff_chunk=ff_chunk, row_groups=2),
        grid=(b, s // tm),
        in_specs=[pl.BlockSpec((None, tm, d), lambda bi, i: (bi, i, 0)),
                  pl.BlockSpec((None, tm // n1, n1 + SUBLANES, FOURIER_WIDTH),
                               lambda bi, i: (bi, i, 0, 0)),
                  pl.BlockSpec((None, tm, ATTN_WIDTH), lambda bi, i: (bi, i, 0)),
                  vec, vec, vec, vec, const((1, d)), const((1, d)),
                  weight((d, d)), weight((d, d_ff)), weight((d, d_ff)), weight((d_ff, d))],
        out_specs=pl.BlockSpec((None, tm, d), lambda bi, i: (bi, i, 0)),
        out_shape=jax.ShapeDtypeStruct((b, s, d), F32),
        compiler_params=pltpu.CompilerParams(dimension_semantics=("parallel", "parallel"),
                                             vmem_limit_bytes=VMEM_LIMIT),
        name="ffn",
    )(x, fm, at, gt1, sh2, sc2, gt2, g_ffn, g_final, w_out, w_gate, w_up, w_down)


def _rope_tables(n_tokens):
    n_freq = HEAD_DIM // 4
    inv_freq = ROPE_THETA ** (-np.arange(n_freq, dtype=np.float64) / n_freq)
    t = np.arange(n_tokens)
    row_ang = (t // GRID_W).astype(np.float64)[:, None] * inv_freq[None, :]
    col_ang = (t % GRID_W).astype(np.float64)[:, None] * inv_freq[None, :]
    ang = np.concatenate([row_ang, row_ang, col_ang, col_ang], axis=1)
    sign = np.tile(np.concatenate([-np.ones(n_freq), np.ones(n_freq)]), 2)
    cos = np.cos(ang)
    sin = np.sin(ang) * sign[None, :]
    reps = LANES // HEAD_DIM
    f32 = lambda a: np.ascontiguousarray(a).astype(np.float32)
    return f32(np.tile(cos, (1, reps))), f32(np.tile(sin, (1, reps))), f32(cos.T), f32(sin.T)


def _head_block_diag(width):
    idx = np.arange(width) // HEAD_DIM
    return (idx[:, None] == idx[None, :]).astype(np.float32)


def _key_block(n_keys):
    for tk in range(256, 0, -LANES):
        if n_keys % tk == 0:
            return tk
    raise ValueError(f"n_keys={n_keys} is not a multiple of {LANES}")


def kernel(x, c, ctx, c_ctx, w_ada, b_ada, g_mix, w_in, w_four, q_gain, k_gain, w_out, g_ffn,
           w_gate, w_up, w_down, g_final):
    b, s, d = x.shape
    n_ctx = ctx.shape[1]
    assert w_ada.shape[0] == 1, "single-layer block"
    assert s % (DFT_INNER * SUBLANES) == 0 and n_ctx % LANES == 0

    pad = (-(b + 1)) % SUBLANES
    cond = jnp.concatenate([c, c_ctx[None, :], jnp.zeros((pad, d), F32)], axis=0)
    ada = _adaln(cond, w_ada[0], b_ada[0][None, :])
    sh1, sc1, gt1, sh2, sc2, gt2 = [ada[:, i * d:(i + 1) * d] for i in range(6)]
    per_batch = lambda m: m[:b, None, :]
    for_ctx = lambda m: jnp.broadcast_to(m[b][None, None, :], (b, 1, d))

    w_in_b = w_in[0].astype(BF16)
    bd = jnp.asarray(_head_block_diag(KV_WIDTH)).astype(BF16)
    qg_t = jnp.broadcast_to((q_gain[0] * Q_SCALE)[:, None], (HEAD_DIM, LANES))
    kg = jnp.tile(k_gain[0], N_KV_HEADS)[None, :]
    g_mix2 = g_mix[0][None, :]

    cos, sin, cos_t, sin_t = [jnp.asarray(t) for t in _rope_tables(s)]
    tm_proj = min(1024, s)
    u, qt, k, vt = _project(x, per_batch(sh1), per_batch(sc1), g_mix2, w_in_b, qg_t, kg, bd,
                            cos, sin, cos_t, sin_t, tm_proj)
    _, _, kc, vtc = _project(ctx, for_ctx(sh1), for_ctx(sc1), g_mix2, w_in_b, qg_t, kg, bd,
                             jnp.ones((n_ctx, LANES), F32), jnp.zeros((n_ctx, LANES), F32),
                             jnp.ones((HEAD_DIM, n_ctx), F32), jnp.zeros((HEAD_DIM, n_ctx), F32), n_ctx)
    n_keys = s + n_ctx
    k_all = jnp.concatenate([k, kc], axis=1)
    vt_all = jnp.concatenate([vt, vtc], axis=2)

    ma, mb, mc = _fourier_tables(s)
    fm = _fourier(u, jnp.asarray(ma).astype(BF16), jnp.asarray(mb).astype(BF16),
                  jnp.asarray(mc), w_four[0], unroll=16)

    at = _attention(qt, k_all, vt_all, q_gain, k_gain, tq=min(512, s), tk=_key_block(n_keys), unroll=True)

    tm_ffn = min(512, s)
    d_ff = w_gate.shape[2]
    ff_chunk = d_ff
    return _ffn(x, fm, at, per_batch(gt1), per_batch(sh2), per_batch(sc2), per_batch(gt2),
                g_ffn[0][None, :], g_final[None, :], w_out[0].astype(BF16), w_gate[0].astype(BF16),
                w_up[0].astype(BF16), w_down[0].astype(BF16), tm_ffn, ff_chunk)
```

```python
import functools
import math

import numpy as np
import jax
import jax.numpy as jnp
from jax import lax
from jax.experimental import pallas as pl
from jax.experimental.pallas import tpu as pltpu

F32 = jnp.float32
BF16 = jnp.bfloat16

EPS = 1e-6
ROPE_THETA = 10000.0
GRID_W = 64
HEAD_DIM = 64
N_HEADS = 8
N_KV_HEADS = 2
Q_PER_KV = N_HEADS // N_KV_HEADS
N_FOURIER_GROUPS = 4
FOURIER_GROUP_DIM = 128
FOURIER_WIDTH = N_FOURIER_GROUPS * FOURIER_GROUP_DIM
ATTN_WIDTH = N_HEADS * HEAD_DIM
KV_WIDTH = N_KV_HEADS * HEAD_DIM

LANES = 128
MXU_COLS = 256
SUBLANES = 8
DFT_INNER = 128
ROW_PITCH = DFT_INNER + SUBLANES
VMEM_LIMIT = 56 * 1024 * 1024

Q_SCALE = (HEAD_DIM ** -0.5) * math.log2(math.e)
SCORE_BOUND_SLACK = 1.02
MAX_UNSHIFTED_SCORE = 64.0


def _dot(a, b):
    return jnp.dot(a, b, preferred_element_type=F32)


def _split_bf16(a):
    hi = a.astype(BF16)
    return hi, (a - hi.astype(F32)).astype(BF16)


def _rms(x):
    return x * lax.rsqrt(jnp.mean(x * x, axis=-1, keepdims=True) + EPS)


def _adaln_kernel(cond_ref, w_ref, b_ref, o_ref):
    c = cond_ref[...]
    s_hi, s_lo = _split_bf16(c / (1.0 + jnp.exp(-c)))
    w_hi, w_lo = _split_bf16(w_ref[...])
    o_ref[...] = _dot(s_hi, w_hi) + _dot(s_lo, w_hi) + _dot(s_hi, w_lo) + b_ref[...]


def _adaln(cond, w, b):
    rows, d = cond.shape
    n = w.shape[1]
    tn = n // 4
    return pl.pallas_call(
        _adaln_kernel,
        grid=(n // tn,),
        in_specs=[pl.BlockSpec((rows, d), lambda j: (0, 0)),
                  pl.BlockSpec((d, tn), lambda j: (0, j)),
                  pl.BlockSpec((1, tn), lambda j: (0, j))],
        out_specs=pl.BlockSpec((rows, tn), lambda j: (0, j)),
        out_shape=jax.ShapeDtypeStruct((rows, n), F32),
        compiler_params=pltpu.CompilerParams(dimension_semantics=("arbitrary",),
                                             vmem_limit_bytes=VMEM_LIMIT),
        name="adaln",
    )(cond, w, b)


def _proj_kernel(x_ref, sh_ref, sc_ref, g_ref, w_ref, qgt_ref, kg_ref, bd_ref, cos_ref, sin_ref,
                 cost_ref, sint_ref, u_ref, qt_ref, k_ref, vt_ref, *, row_groups):
    tm = x_ref.shape[0]
    gm = tm // row_groups
    rows = [slice(i * gm, (i + 1) * gm) for i in range(row_groups)]
    o1 = FOURIER_WIDTH
    o2 = o1 + ATTN_WIDTH
    o3 = o2 + KV_WIDTH
    hs = [((_rms(x_ref[r, :]) * g_ref[...]) * (1.0 + sc_ref[...]) + sh_ref[...]).astype(BF16)
          for r in rows]
    ps = [_dot(h, w_ref[...]) for h in hs]

    lane = lax.broadcasted_iota(jnp.int32, (gm, LANES), 1)
    first_half = (lane & 31) < 16
    gain_t = jnp.tile(qgt_ref[...], (1, gm // LANES))
    quarter = HEAD_DIM // 4

    for r, p in zip(rows, ps):
        u_ref[r, :] = p[:, :o1].astype(u_ref.dtype)
        vt_ref[:, r] = p[:, o3:].T.astype(vt_ref.dtype)

        kx = p[:, o2:o3]
        hi, lo = _split_bf16(kx * kx)
        ss = _dot(hi, bd_ref[...]) + _dot(lo, bd_ref[...])
        kn = kx * lax.rsqrt(ss * (1.0 / HEAD_DIM) + EPS) * kg_ref[...]
        partner = jnp.where(first_half, pltpu.roll(kn, LANES - quarter, axis=1),
                            pltpu.roll(kn, quarter, axis=1))
        k_ref[r, :] = (kn * cos_ref[r, :] + partner * sin_ref[r, :]).astype(k_ref.dtype)

        cos_t = cost_ref[:, r]
        sin_t = sint_ref[:, r]
        for j in range(ATTN_WIDTH // LANES):
            qtj = p[:, o1 + j * LANES:o1 + (j + 1) * LANES].T
            for hh in range(LANES // HEAD_DIM):
                qh = qtj[hh * HEAD_DIM:(hh + 1) * HEAD_DIM]
                ss = jnp.sum(qh * qh, axis=0, keepdims=True)
                qn = qh * lax.rsqrt(ss * (1.0 / HEAD_DIM) + EPS) * gain_t
                partner = jnp.concatenate([qn[quarter:2 * quarter], qn[:quarter],
                                           qn[3 * quarter:], qn[2 * quarter:3 * quarter]], axis=0)
                row0 = j * LANES + hh * HEAD_DIM
                qt_ref[row0:row0 + HEAD_DIM, r] = (qn * cos_t + partner * sin_t).astype(qt_ref.dtype)


def _ada_spec(chunk, d, cond_row=None):
    row = (lambda bi: bi) if cond_row is None else (lambda bi: cond_row)
    return pl.BlockSpec((None, None, 1, d), lambda bi, i: (row(bi), chunk, 0, 0))


ADA_SHIFT1, ADA_SCALE1, ADA_GATE1, ADA_SHIFT2, ADA_SCALE2, ADA_GATE2 = range(6)


def _project(x, ada, cond_row, gain, w_in, q_gain_t, k_gain, bd, cos, sin, cos_t, sin_t, tm):
    b, s, d = x.shape
    in_w = w_in.shape[1]
    const = lambda shape: pl.BlockSpec(shape, lambda bi, i: (0,) * len(shape))
    row_groups = max(1, tm // MXU_COLS)
    return pl.pallas_call(
        functools.partial(_proj_kernel, row_groups=row_groups),
        grid=(b, s // tm),
        in_specs=[pl.BlockSpec((None, tm, d), lambda bi, i: (bi, i, 0)),
                  _ada_spec(ADA_SHIFT1, d, cond_row), _ada_spec(ADA_SCALE1, d, cond_row),
                  const((1, d)), const((d, in_w)),
                  const((HEAD_DIM, LANES)), const((1, KV_WIDTH)), const((KV_WIDTH, KV_WIDTH)),
                  pl.BlockSpec((tm, LANES), lambda bi, i: (i, 0)),
                  pl.BlockSpec((tm, LANES), lambda bi, i: (i, 0)),
                  pl.BlockSpec((HEAD_DIM, tm), lambda bi, i: (0, i)),
                  pl.BlockSpec((HEAD_DIM, tm), lambda bi, i: (0, i))],
        out_specs=[pl.BlockSpec((None, tm, FOURIER_WIDTH), lambda bi, i: (bi, i, 0)),
                   pl.BlockSpec((None, ATTN_WIDTH, tm), lambda bi, i: (bi, 0, i)),
                   pl.BlockSpec((None, tm, KV_WIDTH), lambda bi, i: (bi, i, 0)),
                   pl.BlockSpec((None, KV_WIDTH, tm), lambda bi, i: (bi, 0, i))],
        out_shape=[jax.ShapeDtypeStruct((b, s, FOURIER_WIDTH), BF16),
                   jax.ShapeDtypeStruct((b, ATTN_WIDTH, s), BF16),
                   jax.ShapeDtypeStruct((b, s, KV_WIDTH), BF16),
                   jax.ShapeDtypeStruct((b, KV_WIDTH, s), BF16)],
        compiler_params=pltpu.CompilerParams(dimension_semantics=("parallel", "parallel"),
                                             vmem_limit_bytes=VMEM_LIMIT),
        name="proj",
    )(x, ada, ada, gain, w_in, q_gain_t, k_gain, bd, cos, sin, cos_t, sin_t)


def _fourier_kernel(u_ref, ma_ref, mb_ref, mc_ref, wf_ref, o_ref, zr_sc, zi_sc, tsc, *, unroll):
    n2_count = DFT_INNER
    n1_count = u_ref.shape[0] // n2_count
    c = FOURIER_GROUP_DIM
    t_rows = 2 * n1_count
    t_pitch = t_rows + SUBLANES
    o_pitch = n1_count + SUBLANES

    fw = _dot(mc_ref[...], wf_ref[...]).astype(BF16)
    fw_cat = jnp.concatenate([fw[:c], fw[c:]], axis=1)

    def fill(i, carry):
        src = pl.multiple_of(i * n2_count, n2_count)
        dst = pl.multiple_of(i * ROW_PITCH, SUBLANES)
        z = _dot(u_ref[pl.ds(src, n2_count), :], fw_cat)
        zr_sc[pl.ds(dst, n2_count), :] = z[:, :c]
        zi_sc[pl.ds(dst, n2_count), :] = z[:, c:]
        return carry
    lax.fori_loop(0, n1_count, fill, 0, unroll=min(unroll, n1_count))

    def stage_a(n2, carry):
        zr = zr_sc[pl.ds(n2, n1_count, stride=ROW_PITCH), :]
        zi = zi_sc[pl.ds(n2, n1_count, stride=ROW_PITCH), :]
        z2 = jnp.concatenate([zr, zi], axis=0).astype(BF16)
        dst = pl.multiple_of(n2 * t_pitch, SUBLANES)
        tsc[pl.ds(dst, t_rows), :] = _dot(ma_ref[n2], z2)
        return carry
    lax.fori_loop(0, n2_count, stage_a, 0, unroll=min(2 * unroll, n2_count))

    def stage_b(pair, carry):
        t2 = []
        for k1 in (2 * pair, 2 * pair + 1):
            tr = tsc[pl.ds(k1, n2_count, stride=t_pitch), :]
            ti = tsc[pl.ds(n1_count + k1, n2_count, stride=t_pitch), :]
            t2.append(jnp.concatenate([tr, ti], axis=0).astype(BF16))
        y = _dot(mb_ref[...], jnp.concatenate(t2, axis=1))
        o_ref[pl.ds(2 * pair, n2_count, stride=o_pitch), :] = y[:, :c]
        o_ref[pl.ds(2 * pair + 1, n2_count, stride=o_pitch), :] = y[:, c:]
        return carry
    lax.fori_loop(0, n1_count // 2, stage_b, 0, unroll=min(unroll, n1_count // 2))
    for pad_row in range(SUBLANES):
        o_ref[pl.ds(n1_count + pad_row, n2_count, stride=o_pitch), :] = jnp.zeros((n2_count, c), F32)


def _fourier(u, ma, mb, mc, w_four, unroll):
    b, s, _ = u.shape
    c = FOURIER_GROUP_DIM
    n1 = s // DFT_INNER
    single = pl.Buffered(1)
    return pl.pallas_call(
        functools.partial(_fourier_kernel, unroll=unroll),
        grid=(b, N_FOURIER_GROUPS),
        in_specs=[pl.BlockSpec((None, s, c), lambda bi, g: (bi, 0, g)),
                  pl.BlockSpec(ma.shape, lambda bi, g: (0, 0, 0), pipeline_mode=single),
                  pl.BlockSpec(mb.shape, lambda bi, g: (0, 0), pipeline_mode=single),
                  pl.BlockSpec(mc.shape, lambda bi, g: (0, 0), pipeline_mode=single),
                  pl.BlockSpec((None, c, c), lambda bi, g: (g, 0, 0))],
        out_specs=pl.BlockSpec((None, DFT_INNER * (n1 + SUBLANES), c), lambda bi, g: (bi, 0, g)),
        out_shape=jax.ShapeDtypeStruct((b, DFT_INNER * (n1 + SUBLANES), FOURIER_WIDTH), F32),
        scratch_shapes=[pltpu.VMEM((n1 * ROW_PITCH, c), F32),
                        pltpu.VMEM((n1 * ROW_PITCH, c), F32),
                        pltpu.VMEM((DFT_INNER * (2 * n1 + SUBLANES), c), F32)],
        compiler_params=pltpu.CompilerParams(dimension_semantics=("parallel", "parallel"),
                                             vmem_limit_bytes=VMEM_LIMIT),
        name="fourier",
    )(u, ma, mb, mc, w_four)


def _fourier_tables(s):
    n2c = DFT_INNER
    n1c = s // n2c
    k1 = np.arange(n1c, dtype=np.float64)
    n = (n2c * np.arange(n1c)[None, :] + np.arange(n2c)[:, None]).astype(np.float64)
    ang = 2.0 * np.pi * k1[None, :, None] * n[:, None, :] / s
    ca, sa = np.cos(ang), np.sin(ang)
    ma = np.concatenate([np.concatenate([ca, sa], axis=2),
                         np.concatenate([-sa, ca], axis=2)], axis=1)
    kk = np.arange(n2c, dtype=np.float64)
    a2 = 2.0 * np.pi * np.outer(kk, kk) / n2c
    mb = np.concatenate([np.cos(a2), np.sin(a2)], axis=1)
    cc = np.arange(FOURIER_GROUP_DIM, dtype=np.float64)
    ac = 2.0 * np.pi * np.outer(cc, cc) / FOURIER_GROUP_DIM
    norm = 1.0 / math.sqrt(s * FOURIER_GROUP_DIM)
    mc = np.concatenate([np.cos(ac), -np.sin(ac)], axis=0) * norm
    return (ma.astype(np.float32), mb.astype(np.float32), mc.astype(np.float32))


def _attn_kernel(bound_ref, qt_ref, k_ref, vt_ref, kc_ref, vtc_ref, o_ref, acc_ref, p_ref, l_ref, *, tk):
    tq = qt_ref.shape[1]
    lat_blocks = k_ref.shape[0] // tk
    n_blocks = lat_blocks + kc_ref.shape[0] // tk
    g = pl.program_id(1)
    parts = []
    for h in range(Q_PER_KV):
        qh = qt_ref[h * HEAD_DIM:(h + 1) * HEAD_DIM, :]
        z = jnp.zeros_like(qh)
        parts.append(jnp.where(g == 0, jnp.concatenate([qh, z], axis=0),
                               jnp.concatenate([z, qh], axis=0)))
    w = jnp.concatenate(parts, axis=1)
    rq = w.shape[1]
    acc_ref[...] = jnp.zeros(acc_ref.shape, F32)

    unshifted_ok = bound_ref[0] <= MAX_UNSHIFTED_SCORE

    def blocks(j):
        kr, vr, jj = (k_ref, vt_ref, j) if j < lat_blocks else (kc_ref, vtc_ref, j - lat_blocks)
        return kr[jj * tk:(jj + 1) * tk, :], vr[:, jj * tk:(jj + 1) * tk]

    def sublane_sums(p):
        return p.reshape(p.shape[0] // SUBLANES, SUBLANES, p.shape[1]).sum(axis=0)

    @pl.when(unshifted_ok)
    def _():
        kb0, _ = blocks(0)
        p0 = jnp.exp2(_dot(kb0, w))
        l_ref[...] = sublane_sums(p0)
        p_ref[...] = p0.astype(BF16)

        for j in range(1, n_blocks):
            kb, _ = blocks(j)
            _, vb_prev = blocks(j - 1)
            for c0 in range(0, rq, MXU_COLS):
                cols = slice(c0, c0 + MXU_COLS)
                s = _dot(kb, w[:, cols])
                acc_ref[:, cols] += _dot(vb_prev, p_ref[:, cols])
                p = jnp.exp2(s)
                l_ref[:, cols] += sublane_sums(p)
                p_ref[:, cols] = p.astype(BF16)
        _, vb_last = blocks(n_blocks - 1)
        acc_ref[...] += _dot(vb_last, p_ref[...])

    @pl.when(jnp.logical_not(unshifted_ok))
    def _():
        l_ref[...] = jnp.zeros(l_ref.shape, F32)

        def update(kb, vb, m_prev):
            s = _dot(kb, w)
            m_next = jnp.maximum(m_prev, jnp.max(s, axis=0, keepdims=True))
            alpha = jnp.exp2(m_prev - m_next)
            p = jnp.exp2(s - m_next)
            l_ref[...] = alpha * l_ref[...] + sublane_sums(p)
            acc_ref[...] = alpha * acc_ref[...] + _dot(vb, p.astype(BF16))
            return m_next

        def latent_body(j, m_prev):
            start = pl.multiple_of(j * tk, tk)
            return update(k_ref[pl.ds(start, tk), :], vt_ref[:, pl.ds(start, tk)], m_prev)
        m = lax.fori_loop(0, lat_blocks, latent_body, jnp.full((1, rq), -jnp.inf, F32))
        for j in range(lat_blocks, n_blocks):
            m = update(*blocks(j), m)

    out_t = acc_ref[...] / jnp.sum(l_ref[...], axis=0, keepdims=True)
    out_t = jnp.concatenate([out_t[:, h * tq:(h + 1) * tq] for h in range(Q_PER_KV)], axis=0)
    o_ref[...] = out_t.T.astype(o_ref.dtype)


def _attention(qt, k, vt, kc, vtc, q_gain, k_gain, tq, tk):
    b, _, s = qt.shape
    n_ctx = kc.shape[1]
    assert s % tk == 0 and n_ctx % tk == 0
    gw = Q_PER_KV * HEAD_DIM
    bound = (jnp.max(jnp.abs(q_gain)) * jnp.max(jnp.abs(k_gain))
             * (HEAD_DIM * Q_SCALE * SCORE_BOUND_SLACK)).reshape(1).astype(F32)
    return pl.pallas_call(
        functools.partial(_attn_kernel, tk=tk),
        grid=(b, N_KV_HEADS, s // tq),
        in_specs=[pl.BlockSpec(memory_space=pltpu.SMEM),
                  pl.BlockSpec((None, gw, tq), lambda bi, g, i: (bi, g, i)),
                  pl.BlockSpec((None, s, KV_WIDTH), lambda bi, g, i: (bi, 0, 0)),
                  pl.BlockSpec((None, HEAD_DIM, s), lambda bi, g, i: (bi, g, 0)),
                  pl.BlockSpec((None, n_ctx, KV_WIDTH), lambda bi, g, i: (bi, 0, 0)),
                  pl.BlockSpec((None, HEAD_DIM, n_ctx), lambda bi, g, i: (bi, g, 0))],
        out_specs=pl.BlockSpec((None, tq, gw), lambda bi, g, i: (bi, i, g)),
        out_shape=jax.ShapeDtypeStruct((b, s, ATTN_WIDTH), BF16),
        scratch_shapes=[pltpu.VMEM((HEAD_DIM, Q_PER_KV * tq), F32),
                        pltpu.VMEM((tk, Q_PER_KV * tq), BF16),
                        pltpu.VMEM((SUBLANES, Q_PER_KV * tq), F32)],
        compiler_params=pltpu.CompilerParams(
            dimension_semantics=("parallel", "parallel", "parallel"),
            vmem_limit_bytes=VMEM_LIMIT),
        name="attention",
    )(bound, qt, k, vt, kc, vtc)


def _ffn_kernel(x_ref, fm_ref, at_ref, gt1_ref, sh2_ref, sc2_ref, gt2_ref, gffn_ref, gfin_ref,
                wo_ref, wg_ref, wu_ref, wd_ref, o_ref, *, ff_chunk, row_groups):
    tm = x_ref.shape[0]
    gm = tm // row_groups
    rows = [slice(i * gm, (i + 1) * gm) for i in range(row_groups)]
    n1 = fm_ref.shape[1] - SUBLANES
    k2g = gm // n1
    fms = [fm_ref[i * k2g:(i + 1) * k2g, :n1, :].reshape(gm, FOURIER_WIDTH) for i in range(row_groups)]
    mix = [_dot(f.astype(BF16), wo_ref[:FOURIER_WIDTH, :])
           + _dot(at_ref[r, :], wo_ref[FOURIER_WIDTH:, :]) for f, r in zip(fms, rows)]
    x1 = [x_ref[r, :] + gt1_ref[...] * m for r, m in zip(rows, mix)]
    h2 = [(_rms(x) * gffn_ref[...] * (1.0 + sc2_ref[...]) + sh2_ref[...]).astype(BF16) for x in x1]
    d_ff = wg_ref.shape[1]
    ffn = [None] * row_groups
    for c0 in range(0, d_ff, ff_chunk):
        gate = [_dot(h, wg_ref[:, c0:c0 + ff_chunk]) for h in h2]
        up = [_dot(h, wu_ref[:, c0:c0 + ff_chunk]) for h in h2]
        act = [(g / (1.0 + jnp.exp(-g)) * u).astype(BF16) for g, u in zip(gate, up)]
        for i, a in enumerate(act):
            part = _dot(a, wd_ref[c0:c0 + ff_chunk, :])
            ffn[i] = part if ffn[i] is None else ffn[i] + part
    for r, x, f in zip(rows, x1, ffn):
        o_ref[r, :] = _rms(x + gt2_ref[...] * f) * gfin_ref[...]


def _ffn(x, fm, at, ada, g_ffn, g_final, w_out, w_gate, w_up, w_down, tm, ff_chunk):
    b, s, d = x.shape
    d_ff = w_gate.shape[1]
    n1 = s // DFT_INNER
    fm = fm.reshape(b, DFT_INNER, n1 + SUBLANES, FOURIER_WIDTH)
    single = pl.Buffered(1)
    const = lambda shape: pl.BlockSpec(shape, lambda bi, i: (0,) * len(shape))
    weight = lambda shape: pl.BlockSpec(shape, lambda bi, i: (0,) * len(shape), pipeline_mode=single)
    return pl.pallas_call(
        functools.partial(_ffn_kernel, ff_chunk=ff_chunk, row_groups=max(1, tm // MXU_COLS)),
        grid=(b, s // tm),
        in_specs=[pl.BlockSpec((None, tm, d), lambda bi, i: (bi, i, 0)),
                  pl.BlockSpec((None, tm // n1, n1 + SUBLANES, FOURIER_WIDTH),
                               lambda bi, i: (bi, i, 0, 0)),
                  pl.BlockSpec((None, tm, ATTN_WIDTH), lambda bi, i: (bi, i, 0)),
                  _ada_spec(ADA_GATE1, d), _ada_spec(ADA_SHIFT2, d), _ada_spec(ADA_SCALE2, d),
                  _ada_spec(ADA_GATE2, d), const((1, d)), const((1, d)),
                  weight((d, d)), weight((d, d_ff)), weight((d, d_ff)), weight((d_ff, d))],
        out_specs=pl.BlockSpec((None, tm, d), lambda bi, i: (bi, i, 0)),
        out_shape=jax.ShapeDtypeStruct((b, s, d), F32),
        compiler_params=pltpu.CompilerParams(dimension_semantics=("parallel", "parallel"),
                                             vmem_limit_bytes=VMEM_LIMIT),
        name="ffn",
    )(x, fm, at, ada, ada, ada, ada, g_ffn, g_final, w_out, w_gate, w_up, w_down)


def _rope_tables(n_tokens):
    n_freq = HEAD_DIM // 4
    inv_freq = ROPE_THETA ** (-np.arange(n_freq, dtype=np.float64) / n_freq)
    t = np.arange(n_tokens)
    row_ang = (t // GRID_W).astype(np.float64)[:, None] * inv_freq[None, :]
    col_ang = (t % GRID_W).astype(np.float64)[:, None] * inv_freq[None, :]
    ang = np.concatenate([row_ang, row_ang, col_ang, col_ang], axis=1)
    sign = np.tile(np.concatenate([-np.ones(n_freq), np.ones(n_freq)]), 2)
    cos = np.cos(ang)
    sin = np.sin(ang) * sign[None, :]
    reps = LANES // HEAD_DIM
    f32 = lambda a: np.ascontiguousarray(a).astype(np.float32)
    return f32(np.tile(cos, (1, reps))), f32(np.tile(sin, (1, reps))), f32(cos.T), f32(sin.T)


def _head_block_diag(width):
    idx = np.arange(width) // HEAD_DIM
    return (idx[:, None] == idx[None, :]).astype(np.float32)


def kernel(x, c, ctx, c_ctx, w_ada, b_ada, g_mix, w_in, w_four, q_gain, k_gain, w_out, g_ffn,
           w_gate, w_up, w_down, g_final):
    b, s, d = x.shape
    n_ctx = ctx.shape[1]
    assert w_ada.shape[0] == 1, "single-layer block"
    assert s % (DFT_INNER * SUBLANES) == 0 and n_ctx % LANES == 0

    pad = (-(b + 1)) % SUBLANES
    cond = jnp.concatenate([c, c_ctx[None, :], jnp.zeros((pad, d), F32)], axis=0)
    ada = _adaln(cond, w_ada[0], b_ada[0][None, :]).reshape(b + 1 + pad, 6, 1, d)

    w_in_b = w_in[0].astype(BF16)
    bd = jnp.asarray(_head_block_diag(KV_WIDTH)).astype(BF16)
    qg_t = jnp.broadcast_to((q_gain[0] * Q_SCALE)[:, None], (HEAD_DIM, LANES))
    kg = jnp.tile(k_gain[0], N_KV_HEADS)[None, :]
    g_mix2 = g_mix[0][None, :]

    cos, sin, cos_t, sin_t = [jnp.asarray(t) for t in _rope_tables(s)]
    tm_proj = min(1024, s)
    u, qt, k, vt = _project(x, ada, None, g_mix2, w_in_b, qg_t, kg, bd,
                            cos, sin, cos_t, sin_t, tm_proj)
    _, _, kc, vtc = _project(ctx, ada, b, g_mix2, w_in_b, qg_t, kg, bd,
                             jnp.ones((n_ctx, LANES), F32), jnp.zeros((n_ctx, LANES), F32),
                             jnp.ones((HEAD_DIM, n_ctx), F32), jnp.zeros((HEAD_DIM, n_ctx), F32), n_ctx)
    ma, mb, mc = _fourier_tables(s)
    fm = _fourier(u, jnp.asarray(ma).astype(BF16), jnp.asarray(mb).astype(BF16),
                  jnp.asarray(mc), w_four[0], unroll=16)

    at = _attention(qt, k, vt, kc, vtc, q_gain, k_gain, tq=min(512, s), tk=MXU_COLS)

    tm_ffn = min(512, s)
    d_ff = w_gate.shape[2]
    ff_chunk = d_ff
    return _ffn(x, fm, at, ada, g_ffn[0][None, :], g_final[None, :], w_out[0].astype(BF16), w_gate[0].astype(BF16),
                w_up[0].astype(BF16), w_down[0].astype(BF16), tm_ffn, ff_chunk)
```

```python
import functools
import math

import numpy as np
import jax
import jax.numpy as jnp
from jax import lax
from jax.experimental import pallas as pl
from jax.experimental.pallas import tpu as pltpu

F32 = jnp.float32
BF16 = jnp.bfloat16
F8 = jnp.float8_e4m3fn

EPS = 1e-6
ROPE_THETA = 10000.0
GRID_W = 64
HEAD_DIM = 64
N_HEADS = 8
N_KV_HEADS = 2
Q_PER_KV = N_HEADS // N_KV_HEADS
N_FOURIER_GROUPS = 4
FOURIER_GROUP_DIM = 128
FOURIER_WIDTH = N_FOURIER_GROUPS * FOURIER_GROUP_DIM
ATTN_WIDTH = N_HEADS * HEAD_DIM
KV_WIDTH = N_KV_HEADS * HEAD_DIM
SCORE_DEPTH = 4 * HEAD_DIM
BF16_SUBLANES = 16
VT_ROWS = HEAD_DIM + BF16_SUBLANES

LANES = 128
MXU_COLS = 256
SUBLANES = 8
DFT_INNER = 128
ROW_PITCH = DFT_INNER + SUBLANES
VMEM_LIMIT = 56 * 1024 * 1024

Q_SCALE = (HEAD_DIM ** -0.5) * math.log2(math.e)
SCORE_BOUND_SLACK = 1.02
MAX_UNSHIFTED_SCORE = 64.0


def _dot(a, b):
    return jnp.dot(a, b, preferred_element_type=F32)


def _split_bf16(a):
    hi = a.astype(BF16)
    return hi, (a - hi.astype(F32)).astype(BF16)


def _split_f8(a):
    hi = a.astype(F8)
    return hi, (a - hi.astype(F32)).astype(F8)


def _rms(x):
    return x * lax.rsqrt(jnp.mean(x * x, axis=-1, keepdims=True) + EPS)


def _adaln_kernel(cond_ref, w_ref, b_ref, o_ref):
    c = cond_ref[...]
    s_hi, s_lo = _split_bf16(c / (1.0 + jnp.exp(-c)))
    w_hi, w_lo = _split_bf16(w_ref[...])
    o_ref[...] = _dot(s_hi, w_hi) + _dot(s_lo, w_hi) + _dot(s_hi, w_lo) + b_ref[...]


def _adaln(cond, w, b):
    rows, d = cond.shape
    n = w.shape[1]
    tn = n // 4
    return pl.pallas_call(
        _adaln_kernel,
        grid=(n // tn,),
        in_specs=[pl.BlockSpec((rows, d), lambda j: (0, 0)),
                  pl.BlockSpec((d, tn), lambda j: (0, j)),
                  pl.BlockSpec((1, tn), lambda j: (0, j))],
        out_specs=pl.BlockSpec((rows, tn), lambda j: (0, j)),
        out_shape=jax.ShapeDtypeStruct((rows, n), F32),
        compiler_params=pltpu.CompilerParams(dimension_semantics=("arbitrary",),
                                             vmem_limit_bytes=VMEM_LIMIT),
        name="adaln",
    )(cond, w, b)


def _proj_kernel(x_ref, sh_ref, sc_ref, g_ref, w_ref, qgt_ref, kg_ref, bd_ref, cos_ref, sin_ref,
                 cost_ref, sint_ref, u_ref, qt_ref, k_ref, vt_ref, *, row_groups):
    tm = x_ref.shape[0]
    gm = tm // row_groups
    rows = [slice(i * gm, (i + 1) * gm) for i in range(row_groups)]
    o1 = FOURIER_WIDTH
    o2 = o1 + ATTN_WIDTH
    o3 = o2 + KV_WIDTH
    hs = [((_rms(x_ref[r, :]) * g_ref[...]) * (1.0 + sc_ref[...]) + sh_ref[...]).astype(BF16)
          for r in rows]
    ps = [_dot(h, w_ref[...]) for h in hs]

    lane = lax.broadcasted_iota(jnp.int32, (gm, LANES), 1)
    first_half = (lane & 31) < 16
    gain_t = jnp.tile(qgt_ref[...], (1, gm // LANES))
    quarter = HEAD_DIM // 4
    extra = VT_ROWS - HEAD_DIM
    ones_rows = jnp.where(lax.broadcasted_iota(jnp.int32, (extra, gm), 0) == 0, 1.0, 0.0).astype(vt_ref.dtype)

    for r, p in zip(rows, ps):
        u_ref[r, :] = p[:, :o1].astype(u_ref.dtype)
        v_t = p[:, o3:].T.astype(vt_ref.dtype)
        for kvh in range(N_KV_HEADS):
            vt_ref[kvh * VT_ROWS:kvh * VT_ROWS + HEAD_DIM, r] = v_t[kvh * HEAD_DIM:(kvh + 1) * HEAD_DIM]
            vt_ref[kvh * VT_ROWS + HEAD_DIM:(kvh + 1) * VT_ROWS, r] = ones_rows

        kx = p[:, o2:o3]
        hi, lo = _split_bf16(kx * kx)
        ss = _dot(hi, bd_ref[...]) + _dot(lo, bd_ref[...])
        kn = kx * lax.rsqrt(ss * (1.0 / HEAD_DIM) + EPS) * kg_ref[...]
        partner = jnp.where(first_half, pltpu.roll(kn, LANES - quarter, axis=1),
                            pltpu.roll(kn, quarter, axis=1))
        k_rot = kn * cos_ref[r, :] + partner * sin_ref[r, :]
        k_hi = k_rot.astype(F8).astype(F32)
        k_lo = k_rot - k_hi
        low_lanes = lane < HEAD_DIM
        per_head = (jnp.where(low_lanes, k_hi, pltpu.roll(k_lo, HEAD_DIM, axis=1)),
                    jnp.where(low_lanes, pltpu.roll(k_hi, HEAD_DIM, axis=1), k_lo))
        for kvh, kh in enumerate(per_head):
            kh8 = kh.astype(F8)
            k_ref[kvh, r, :LANES] = kh8
            k_ref[kvh, r, LANES:] = kh8

        cos_t = cost_ref[:, r]
        sin_t = sint_ref[:, r]
        for j in range(ATTN_WIDTH // LANES):
            qtj = p[:, o1 + j * LANES:o1 + (j + 1) * LANES].T
            for hh in range(LANES // HEAD_DIM):
                qh = qtj[hh * HEAD_DIM:(hh + 1) * HEAD_DIM]
                ss = jnp.sum(qh * qh, axis=0, keepdims=True)
                qn = qh * lax.rsqrt(ss * (1.0 / HEAD_DIM) + EPS) * gain_t
                partner = jnp.concatenate([qn[quarter:2 * quarter], qn[:quarter],
                                           qn[3 * quarter:], qn[2 * quarter:3 * quarter]], axis=0)
                q_hi, q_lo = _split_f8(qn * cos_t + partner * sin_t)
                row0 = (j * (LANES // HEAD_DIM) + hh) * SCORE_DEPTH
                for part, val in enumerate((q_hi, q_hi, q_lo, q_lo)):
                    qt_ref[row0 + part * HEAD_DIM:row0 + (part + 1) * HEAD_DIM, r] = val


def _ada_spec(chunk, d, cond_row=None):
    row = (lambda bi: bi) if cond_row is None else (lambda bi: cond_row)
    return pl.BlockSpec((None, None, 1, d), lambda bi, i: (row(bi), chunk, 0, 0))


ADA_SHIFT1, ADA_SCALE1, ADA_GATE1, ADA_SHIFT2, ADA_SCALE2, ADA_GATE2 = range(6)


def _project(x, ada, cond_row, gain, w_in, q_gain_t, k_gain, bd, cos, sin, cos_t, sin_t, tm):
    b, s, d = x.shape
    in_w = w_in.shape[1]
    const = lambda shape: pl.BlockSpec(shape, lambda bi, i: (0,) * len(shape))
    row_groups = max(1, tm // MXU_COLS)
    return pl.pallas_call(
        functools.partial(_proj_kernel, row_groups=row_groups),
        grid=(b, s // tm),
        in_specs=[pl.BlockSpec((None, tm, d), lambda bi, i: (bi, i, 0)),
                  _ada_spec(ADA_SHIFT1, d, cond_row), _ada_spec(ADA_SCALE1, d, cond_row),
                  const((1, d)), const((d, in_w)),
                  const((HEAD_DIM, LANES)), const((1, KV_WIDTH)), const((KV_WIDTH, KV_WIDTH)),
                  pl.BlockSpec((tm, LANES), lambda bi, i: (i, 0)),
                  pl.BlockSpec((tm, LANES), lambda bi, i: (i, 0)),
                  pl.BlockSpec((HEAD_DIM, tm), lambda bi, i: (0, i)),
                  pl.BlockSpec((HEAD_DIM, tm), lambda bi, i: (0, i))],
        out_specs=[pl.BlockSpec((None, tm, FOURIER_WIDTH), lambda bi, i: (bi, i, 0)),
                   pl.BlockSpec((None, N_HEADS * SCORE_DEPTH, tm), lambda bi, i: (bi, 0, i)),
                   pl.BlockSpec((None, N_KV_HEADS, tm, SCORE_DEPTH), lambda bi, i: (bi, 0, i, 0)),
                   pl.BlockSpec((None, N_KV_HEADS * VT_ROWS, tm), lambda bi, i: (bi, 0, i))],
        out_shape=[jax.ShapeDtypeStruct((b, s, FOURIER_WIDTH), BF16),
                   jax.ShapeDtypeStruct((b, N_HEADS * SCORE_DEPTH, s), F8),
                   jax.ShapeDtypeStruct((b, N_KV_HEADS, s, SCORE_DEPTH), F8),
                   jax.ShapeDtypeStruct((b, N_KV_HEADS * VT_ROWS, s), BF16)],
        compiler_params=pltpu.CompilerParams(dimension_semantics=("parallel", "parallel"),
                                             vmem_limit_bytes=VMEM_LIMIT),
        name="proj",
    )(x, ada, ada, gain, w_in, q_gain_t, k_gain, bd, cos, sin, cos_t, sin_t)


def _fourier_kernel(u_ref, ma_ref, mb_ref, mc_ref, wf_ref, o_ref, zr_sc, zi_sc, tsc, *, unroll):
    n2_count = DFT_INNER
    n1_count = u_ref.shape[0] // n2_count
    c = FOURIER_GROUP_DIM
    t_rows = 2 * n1_count
    t_pitch = t_rows + SUBLANES
    o_pitch = n1_count + SUBLANES

    fw = _dot(mc_ref[...], wf_ref[...]).astype(BF16)
    fw_cat = jnp.concatenate([fw[:c], fw[c:]], axis=1)

    def fill(i, carry):
        src = pl.multiple_of(i * n2_count, n2_count)
        dst = pl.multiple_of(i * ROW_PITCH, SUBLANES)
        z = _dot(u_ref[pl.ds(src, n2_count), :], fw_cat)
        zr_sc[pl.ds(dst, n2_count), :] = z[:, :c]
        zi_sc[pl.ds(dst, n2_count), :] = z[:, c:]
        return carry
    lax.fori_loop(0, n1_count, fill, 0, unroll=min(unroll, n1_count))

    def stage_a(n2, carry):
        zr = zr_sc[pl.ds(n2, n1_count, stride=ROW_PITCH), :]
        zi = zi_sc[pl.ds(n2, n1_count, stride=ROW_PITCH), :]
        z2 = jnp.concatenate([zr, zi], axis=0).astype(BF16)
        dst = pl.multiple_of(n2 * t_pitch, SUBLANES)
        tsc[pl.ds(dst, t_rows), :] = _dot(ma_ref[n2], z2)
        return carry
    lax.fori_loop(0, n2_count, stage_a, 0, unroll=min(2 * unroll, n2_count))

    def stage_b(pair, carry):
        t2 = []
        for k1 in (2 * pair, 2 * pair + 1):
            tr = tsc[pl.ds(k1, n2_count, stride=t_pitch), :]
            ti = tsc[pl.ds(n1_count + k1, n2_count, stride=t_pitch), :]
            t2.append(jnp.concatenate([tr, ti], axis=0).astype(BF16))
        y = _dot(mb_ref[...], jnp.concatenate(t2, axis=1))
        o_ref[pl.ds(2 * pair, n2_count, stride=o_pitch), :] = y[:, :c]
        o_ref[pl.ds(2 * pair + 1, n2_count, stride=o_pitch), :] = y[:, c:]
        return carry
    lax.fori_loop(0, n1_count // 2, stage_b, 0, unroll=min(unroll, n1_count // 2))
    for pad_row in range(SUBLANES):
        o_ref[pl.ds(n1_count + pad_row, n2_count, stride=o_pitch), :] = jnp.zeros((n2_count, c), F32)


def _fourier(u, ma, mb, mc, w_four, unroll):
    b, s, _ = u.shape
    c = FOURIER_GROUP_DIM
    n1 = s // DFT_INNER
    single = pl.Buffered(1)
    return pl.pallas_call(
        functools.partial(_fourier_kernel, unroll=unroll),
        grid=(b, N_FOURIER_GROUPS),
        in_specs=[pl.BlockSpec((None, s, c), lambda bi, g: (bi, 0, g)),
                  pl.BlockSpec(ma.shape, lambda bi, g: (0, 0, 0), pipeline_mode=single),
                  pl.BlockSpec(mb.shape, lambda bi, g: (0, 0), pipeline_mode=single),
                  pl.BlockSpec(mc.shape, lambda bi, g: (0, 0), pipeline_mode=single),
                  pl.BlockSpec((None, c, c), lambda bi, g: (g, 0, 0))],
        out_specs=pl.BlockSpec((None, DFT_INNER * (n1 + SUBLANES), c), lambda bi, g: (bi, 0, g)),
        out_shape=jax.ShapeDtypeStruct((b, DFT_INNER * (n1 + SUBLANES), FOURIER_WIDTH), F32),
        scratch_shapes=[pltpu.VMEM((n1 * ROW_PITCH, c), F32),
                        pltpu.VMEM((n1 * ROW_PITCH, c), F32),
                        pltpu.VMEM((DFT_INNER * (2 * n1 + SUBLANES), c), F32)],
        compiler_params=pltpu.CompilerParams(dimension_semantics=("parallel", "parallel"),
                                             vmem_limit_bytes=VMEM_LIMIT),
        name="fourier",
    )(u, ma, mb, mc, w_four)


def _fourier_tables(s):
    n2c = DFT_INNER
    n1c = s // n2c
    k1 = np.arange(n1c, dtype=np.float64)
    n = (n2c * np.arange(n1c)[None, :] + np.arange(n2c)[:, None]).astype(np.float64)
    ang = 2.0 * np.pi * k1[None, :, None] * n[:, None, :] / s
    ca, sa = np.cos(ang), np.sin(ang)
    ma = np.concatenate([np.concatenate([ca, sa], axis=2),
                         np.concatenate([-sa, ca], axis=2)], axis=1)
    kk = np.arange(n2c, dtype=np.float64)
    a2 = 2.0 * np.pi * np.outer(kk, kk) / n2c
    mb = np.concatenate([np.cos(a2), np.sin(a2)], axis=1)
    cc = np.arange(FOURIER_GROUP_DIM, dtype=np.float64)
    ac = 2.0 * np.pi * np.outer(cc, cc) / FOURIER_GROUP_DIM
    norm = 1.0 / math.sqrt(s * FOURIER_GROUP_DIM)
    mc = np.concatenate([np.cos(ac), -np.sin(ac)], axis=0) * norm
    return (ma.astype(np.float32), mb.astype(np.float32), mc.astype(np.float32))


def _attn_kernel(scal_ref, qt_ref, k_ref, vt_ref, kc_ref, vtc_ref, o_ref, acc_ref, p_ref, *, tk):
    tq = qt_ref.shape[1]
    lat_blocks = k_ref.shape[0] // tk
    n_blocks = lat_blocks + kc_ref.shape[0] // tk
    w = jnp.concatenate([qt_ref[h * SCORE_DEPTH:(h + 1) * SCORE_DEPTH, :] for h in range(Q_PER_KV)],
                        axis=1)
    rq = w.shape[1]
    acc_ref[...] = jnp.zeros(acc_ref.shape, F32)

    unshifted_ok = scal_ref[0] <= MAX_UNSHIFTED_SCORE

    def blocks(j):
        kr, vr, jj = (k_ref, vt_ref, j) if j < lat_blocks else (kc_ref, vtc_ref, j - lat_blocks)
        return kr[jj * tk:(jj + 1) * tk, :], vr[:, jj * tk:(jj + 1) * tk]

    @pl.when(unshifted_ok)
    def _():
        kb0, _ = blocks(0)
        p_ref[...] = jnp.exp2(_dot(kb0, w)).astype(BF16)

        for j in range(1, n_blocks):
            kb, _ = blocks(j)
            _, vb_prev = blocks(j - 1)
            for c0 in range(0, rq, MXU_COLS):
                cols = slice(c0, c0 + MXU_COLS)
                s = _dot(kb, w[:, cols])
                acc_ref[:, cols] += _dot(vb_prev, p_ref[:, cols])
                p_ref[:, cols] = jnp.exp2(s).astype(BF16)
        _, vb_last = blocks(n_blocks - 1)
        acc_ref[...] += _dot(vb_last, p_ref[...])

    @pl.when(jnp.logical_not(unshifted_ok))
    def _():
        score_scale = scal_ref[1]

        def update(kb, vb, m_prev):
            s = _dot(kb, w) * score_scale
            m_next = jnp.maximum(m_prev, jnp.max(s, axis=0, keepdims=True))
            alpha = jnp.exp2(m_prev - m_next)
            p = jnp.exp2(s - m_next).astype(BF16)
            acc_ref[...] = alpha * acc_ref[...] + _dot(vb, p)
            return m_next

        def latent_body(j, m_prev):
            start = pl.multiple_of(j * tk, tk)
            return update(k_ref[pl.ds(start, tk), :], vt_ref[:, pl.ds(start, tk)], m_prev)
        m = lax.fori_loop(0, lat_blocks, latent_body, jnp.full((1, rq), -jnp.inf, F32))
        for j in range(lat_blocks, n_blocks):
            m = update(*blocks(j), m)

    acc = acc_ref[...]
    out_t = acc[:HEAD_DIM] / acc[HEAD_DIM:HEAD_DIM + 1]
    out_t = jnp.concatenate([out_t[:, h * tq:(h + 1) * tq] for h in range(Q_PER_KV)], axis=0)
    o_ref[...] = out_t.T.astype(o_ref.dtype)


def _attention(qt, k, vt, kc, vtc, score_bound, score_scale, tq, tk):
    b, _, s = qt.shape
    n_ctx = kc.shape[2]
    assert s % tk == 0 and n_ctx % tk == 0
    gw = Q_PER_KV * HEAD_DIM
    scal = jnp.stack([score_bound, score_scale]).astype(F32)
    return pl.pallas_call(
        functools.partial(_attn_kernel, tk=tk),
        grid=(b, N_KV_HEADS, s // tq),
        in_specs=[pl.BlockSpec(memory_space=pltpu.SMEM),
                  pl.BlockSpec((None, Q_PER_KV * SCORE_DEPTH, tq), lambda bi, g, i: (bi, g, i)),
                  pl.BlockSpec((None, None, s, SCORE_DEPTH), lambda bi, g, i: (bi, g, 0, 0)),
                  pl.BlockSpec((None, VT_ROWS, s), lambda bi, g, i: (bi, g, 0)),
                  pl.BlockSpec((None, None, n_ctx, SCORE_DEPTH), lambda bi, g, i: (bi, g, 0, 0)),
                  pl.BlockSpec((None, VT_ROWS, n_ctx), lambda bi, g, i: (bi, g, 0))],
        out_specs=pl.BlockSpec((None, tq, gw), lambda bi, g, i: (bi, i, g)),
        out_shape=jax.ShapeDtypeStruct((b, s, ATTN_WIDTH), BF16),
        scratch_shapes=[pltpu.VMEM((VT_ROWS, Q_PER_KV * tq), F32),
                        pltpu.VMEM((tk, Q_PER_KV * tq), BF16)],
        compiler_params=pltpu.CompilerParams(
            dimension_semantics=("parallel", "parallel", "parallel"),
            vmem_limit_bytes=VMEM_LIMIT),
        name="attention",
    )(scal, qt, k, vt, kc, vtc)


def _ffn_kernel(x_ref, fm_ref, at_ref, gt1_ref, sh2_ref, sc2_ref, gt2_ref, gffn_ref, gfin_ref,
                wo_ref, wg_ref, wu_ref, wd_ref, o_ref, *, ff_chunk, row_groups):
    tm = x_ref.shape[0]
    gm = tm // row_groups
    rows = [slice(i * gm, (i + 1) * gm) for i in range(row_groups)]
    n1 = fm_ref.shape[1] - SUBLANES
    k2g = gm // n1
    fms = [fm_ref[i * k2g:(i + 1) * k2g, :n1, :].reshape(gm, FOURIER_WIDTH) for i in range(row_groups)]
    mix = [_dot(f.astype(BF16), wo_ref[:FOURIER_WIDTH, :])
           + _dot(at_ref[r, :], wo_ref[FOURIER_WIDTH:, :]) for f, r in zip(fms, rows)]
    x1 = [x_ref[r, :] + gt1_ref[...] * m for r, m in zip(rows, mix)]
    h2 = [(_rms(x) * gffn_ref[...] * (1.0 + sc2_ref[...]) + sh2_ref[...]).astype(BF16) for x in x1]
    d_ff = wg_ref.shape[1]
    ffn = [None] * row_groups
    for c0 in range(0, d_ff, ff_chunk):
        gate = [_dot(h, wg_ref[:, c0:c0 + ff_chunk]) for h in h2]
        up = [_dot(h, wu_ref[:, c0:c0 + ff_chunk]) for h in h2]
        act = [(g / (1.0 + jnp.exp(-g)) * u).astype(BF16) for g, u in zip(gate, up)]
        for i, a in enumerate(act):
            part = _dot(a, wd_ref[c0:c0 + ff_chunk, :])
            ffn[i] = part if ffn[i] is None else ffn[i] + part
    for r, x, f in zip(rows, x1, ffn):
        o_ref[r, :] = _rms(x + gt2_ref[...] * f) * gfin_ref[...]


def _ffn(x, fm, at, ada, g_ffn, g_final, w_out, w_gate, w_up, w_down, tm, ff_chunk):
    b, s, d = x.shape
    d_ff = w_gate.shape[1]
    n1 = s // DFT_INNER
    fm = fm.reshape(b, DFT_INNER, n1 + SUBLANES, FOURIER_WIDTH)
    single = pl.Buffered(1)
    const = lambda shape: pl.BlockSpec(shape, lambda bi, i: (0,) * len(shape))
    weight = lambda shape: pl.BlockSpec(shape, lambda bi, i: (0,) * len(shape), pipeline_mode=single)
    return pl.pallas_call(
        functools.partial(_ffn_kernel, ff_chunk=ff_chunk, row_groups=max(1, tm // MXU_COLS)),
        grid=(b, s // tm),
        in_specs=[pl.BlockSpec((None, tm, d), lambda bi, i: (bi, i, 0)),
                  pl.BlockSpec((None, tm // n1, n1 + SUBLANES, FOURIER_WIDTH),
                               lambda bi, i: (bi, i, 0, 0)),
                  pl.BlockSpec((None, tm, ATTN_WIDTH), lambda bi, i: (bi, i, 0)),
                  _ada_spec(ADA_GATE1, d), _ada_spec(ADA_SHIFT2, d), _ada_spec(ADA_SCALE2, d),
                  _ada_spec(ADA_GATE2, d), const((1, d)), const((1, d)),
                  weight((d, d)), weight((d, d_ff)), weight((d, d_ff)), weight((d_ff, d))],
        out_specs=pl.BlockSpec((None, tm, d), lambda bi, i: (bi, i, 0)),
        out_shape=jax.ShapeDtypeStruct((b, s, d), F32),
        compiler_params=pltpu.CompilerParams(dimension_semantics=("parallel", "parallel"),
                                             vmem_limit_bytes=VMEM_LIMIT),
        name="ffn",
    )(x, fm, at, ada, ada, ada, ada, g_ffn, g_final, w_out, w_gate, w_up, w_down)


def _rope_tables(n_tokens):
    n_freq = HEAD_DIM // 4
    inv_freq = ROPE_THETA ** (-np.arange(n_freq, dtype=np.float64) / n_freq)
    t = np.arange(n_tokens)
    row_ang = (t // GRID_W).astype(np.float64)[:, None] * inv_freq[None, :]
    col_ang = (t % GRID_W).astype(np.float64)[:, None] * inv_freq[None, :]
    ang = np.concatenate([row_ang, row_ang, col_ang, col_ang], axis=1)
    sign = np.tile(np.concatenate([-np.ones(n_freq), np.ones(n_freq)]), 2)
    cos = np.cos(ang)
    sin = np.sin(ang) * sign[None, :]
    reps = LANES // HEAD_DIM
    f32 = lambda a: np.ascontiguousarray(a).astype(np.float32)
    return f32(np.tile(cos, (1, reps))), f32(np.tile(sin, (1, reps))), f32(cos.T), f32(sin.T)


def _head_block_diag(width):
    idx = np.arange(width) // HEAD_DIM
    return (idx[:, None] == idx[None, :]).astype(np.float32)


def kernel(x, c, ctx, c_ctx, w_ada, b_ada, g_mix, w_in, w_four, q_gain, k_gain, w_out, g_ffn,
           w_gate, w_up, w_down, g_final):
    b, s, d = x.shape
    n_ctx = ctx.shape[1]
    assert w_ada.shape[0] == 1, "single-layer block"
    assert s % (DFT_INNER * SUBLANES) == 0 and n_ctx % LANES == 0

    pad = (-(b + 1)) % SUBLANES
    cond = jnp.concatenate([c, c_ctx[None, :], jnp.zeros((pad, d), F32)], axis=0)
    ada = _adaln(cond, w_ada[0], b_ada[0][None, :]).reshape(b + 1 + pad, 6, 1, d)

    w_in_b = w_in[0].astype(BF16)
    bd = jnp.asarray(_head_block_diag(KV_WIDTH)).astype(BF16)
    gq_max = jnp.maximum(jnp.max(jnp.abs(q_gain[0])), jnp.finfo(F32).tiny)
    gk_max = jnp.maximum(jnp.max(jnp.abs(k_gain[0])), jnp.finfo(F32).tiny)
    score_scale = Q_SCALE * gq_max * gk_max
    score_bound = score_scale * (HEAD_DIM * SCORE_BOUND_SLACK)
    fold = score_bound <= MAX_UNSHIFTED_SCORE
    k_fold = jnp.where(fold, score_scale, 1.0)
    qg_t = jnp.broadcast_to((q_gain[0] / gq_max)[:, None], (HEAD_DIM, LANES))
    kg = jnp.tile(k_gain[0] * (k_fold / gk_max), N_KV_HEADS)[None, :]
    g_mix2 = g_mix[0][None, :]

    cos, sin, cos_t, sin_t = [jnp.asarray(t) for t in _rope_tables(s)]
    tm_proj = min(1024, s)
    u, qt, k, vt = _project(x, ada, None, g_mix2, w_in_b, qg_t, kg, bd,
                            cos, sin, cos_t, sin_t, tm_proj)
    _, _, kc, vtc = _project(ctx, ada, b, g_mix2, w_in_b, qg_t, kg, bd,
                             jnp.ones((n_ctx, LANES), F32), jnp.zeros((n_ctx, LANES), F32),
                             jnp.ones((HEAD_DIM, n_ctx), F32), jnp.zeros((HEAD_DIM, n_ctx), F32), n_ctx)
    ma, mb, mc = _fourier_tables(s)
    fm = _fourier(u, jnp.asarray(ma).astype(BF16), jnp.asarray(mb).astype(BF16),
                  jnp.asarray(mc), w_four[0], unroll=32)

    at = _attention(qt, k, vt, kc, vtc, score_bound, score_scale / k_fold, tq=min(512, s), tk=MXU_COLS)

    tm_ffn = min(512, s)
    d_ff = w_gate.shape[2]
    ff_chunk = d_ff
    return _ffn(x, fm, at, ada, g_ffn[0][None, :], g_final[None, :], w_out[0].astype(BF16),
                w_gate[0].astype(BF16), w_up[0].astype(BF16), w_down[0].astype(BF16), tm_ffn, ff_chunk)
```

```python
import functools
import math

import numpy as np
import jax
import jax.numpy as jnp
from jax import lax
from jax.experimental import pallas as pl
from jax.experimental.pallas import tpu as pltpu

F32 = jnp.float32
BF16 = jnp.bfloat16
F8 = jnp.float8_e4m3fn

EPS = 1e-6
ROPE_THETA = 10000.0
GRID_W = 64
HEAD_DIM = 64
N_HEADS = 8
N_KV_HEADS = 2
Q_PER_KV = N_HEADS // N_KV_HEADS
N_FOURIER_GROUPS = 4
FOURIER_GROUP_DIM = 128
FOURIER_WIDTH = N_FOURIER_GROUPS * FOURIER_GROUP_DIM
ATTN_WIDTH = N_HEADS * HEAD_DIM
KV_WIDTH = N_KV_HEADS * HEAD_DIM
SCORE_DEPTH = 4 * HEAD_DIM
BF16_SUBLANES = 16
VT_ROWS = HEAD_DIM + BF16_SUBLANES

LANES = 128
MXU_COLS = 256
SUBLANES = 8
DFT_INNER = 128
ROW_PITCH = DFT_INNER + SUBLANES
VMEM_LIMIT = 56 * 1024 * 1024

Q_SCALE = (HEAD_DIM ** -0.5) * math.log2(math.e)
SCORE_BOUND_SLACK = 1.02
MAX_UNSHIFTED_SCORE = 64.0


def _dot(a, b):
    return jnp.dot(a, b, preferred_element_type=F32)


def _split_bf16(a):
    hi = a.astype(BF16)
    return hi, (a - hi.astype(F32)).astype(BF16)


def _split_f8(a):
    hi = a.astype(F8)
    return hi, (a - hi.astype(F32)).astype(F8)


def _rms(x):
    return x * lax.rsqrt(jnp.mean(x * x, axis=-1, keepdims=True) + EPS)


def _adaln_kernel(cond_ref, w_ref, b_ref, o_ref):
    c = cond_ref[...]
    s_hi, s_lo = _split_bf16(c / (1.0 + jnp.exp(-c)))
    w_hi, w_lo = _split_bf16(w_ref[...])
    o_ref[...] = _dot(s_hi, w_hi) + _dot(s_lo, w_hi) + _dot(s_hi, w_lo) + b_ref[...]


def _adaln(cond, w, b):
    rows, d = cond.shape
    n = w.shape[1]
    tn = n // 4
    return pl.pallas_call(
        _adaln_kernel,
        grid=(n // tn,),
        in_specs=[pl.BlockSpec((rows, d), lambda j: (0, 0)),
                  pl.BlockSpec((d, tn), lambda j: (0, j)),
                  pl.BlockSpec((1, tn), lambda j: (0, j))],
        out_specs=pl.BlockSpec((rows, tn), lambda j: (0, j)),
        out_shape=jax.ShapeDtypeStruct((rows, n), F32),
        compiler_params=pltpu.CompilerParams(dimension_semantics=("arbitrary",),
                                             vmem_limit_bytes=VMEM_LIMIT),
        name="adaln",
    )(cond, w, b)


def _proj_kernel(x_ref, sh_ref, sc_ref, g_ref, w_ref, qgt_ref, kg_ref, bd_ref, cos_ref, sin_ref,
                 cost_ref, sint_ref, u_ref, qt_ref, k_ref, vt_ref, *, row_groups):
    tm = x_ref.shape[0]
    gm = tm // row_groups
    rows = [slice(i * gm, (i + 1) * gm) for i in range(row_groups)]
    o1 = FOURIER_WIDTH
    o2 = o1 + ATTN_WIDTH
    o3 = o2 + KV_WIDTH
    hs = [((_rms(x_ref[r, :]) * g_ref[...]) * (1.0 + sc_ref[...]) + sh_ref[...]).astype(BF16)
          for r in rows]
    ps = [_dot(h, w_ref[...]) for h in hs]

    lane = lax.broadcasted_iota(jnp.int32, (gm, LANES), 1)
    first_half = (lane & 31) < 16
    gain_t = jnp.tile(qgt_ref[...], (1, gm // LANES))
    quarter = HEAD_DIM // 4
    extra = VT_ROWS - HEAD_DIM
    ones_rows = jnp.where(lax.broadcasted_iota(jnp.int32, (extra, gm), 0) == 0, 1.0, 0.0).astype(vt_ref.dtype)

    for r, p in zip(rows, ps):
        u_ref[r, :] = p[:, :o1].astype(u_ref.dtype)
        v_t = p[:, o3:].T.astype(vt_ref.dtype)
        for kvh in range(N_KV_HEADS):
            vt_ref[kvh * VT_ROWS:kvh * VT_ROWS + HEAD_DIM, r] = v_t[kvh * HEAD_DIM:(kvh + 1) * HEAD_DIM]
            vt_ref[kvh * VT_ROWS + HEAD_DIM:(kvh + 1) * VT_ROWS, r] = ones_rows

        kx = p[:, o2:o3]
        hi, lo = _split_bf16(kx * kx)
        ss = _dot(hi, bd_ref[...]) + _dot(lo, bd_ref[...])
        kn = kx * lax.rsqrt(ss * (1.0 / HEAD_DIM) + EPS) * kg_ref[...]
        partner = jnp.where(first_half, pltpu.roll(kn, LANES - quarter, axis=1),
                            pltpu.roll(kn, quarter, axis=1))
        k_rot = kn * cos_ref[r, :] + partner * sin_ref[r, :]
        k_hi = k_rot.astype(F8).astype(F32)
        k_lo = k_rot - k_hi
        low_lanes = lane < HEAD_DIM
        per_head = (jnp.where(low_lanes, k_hi, pltpu.roll(k_lo, HEAD_DIM, axis=1)),
                    jnp.where(low_lanes, pltpu.roll(k_hi, HEAD_DIM, axis=1), k_lo))
        for kvh, kh in enumerate(per_head):
            kh8 = kh.astype(F8)
            k_ref[kvh, r, :LANES] = kh8
            k_ref[kvh, r, LANES:] = kh8

        cos_t = cost_ref[:, r]
        sin_t = sint_ref[:, r]
        for j in range(ATTN_WIDTH // LANES):
            qtj = p[:, o1 + j * LANES:o1 + (j + 1) * LANES].T
            for hh in range(LANES // HEAD_DIM):
                qh = qtj[hh * HEAD_DIM:(hh + 1) * HEAD_DIM]
                ss = jnp.sum(qh * qh, axis=0, keepdims=True)
                qn = qh * lax.rsqrt(ss * (1.0 / HEAD_DIM) + EPS) * gain_t
                partner = jnp.concatenate([qn[quarter:2 * quarter], qn[:quarter],
                                           qn[3 * quarter:], qn[2 * quarter:3 * quarter]], axis=0)
                q_hi, q_lo = _split_f8(qn * cos_t + partner * sin_t)
                row0 = (j * (LANES // HEAD_DIM) + hh) * SCORE_DEPTH
                for part, val in enumerate((q_hi, q_hi, q_lo, q_lo)):
                    qt_ref[row0 + part * HEAD_DIM:row0 + (part + 1) * HEAD_DIM, r] = val


def _ada_spec(chunk, d, cond_row=None):
    row = (lambda bi: bi) if cond_row is None else (lambda bi: cond_row)
    return pl.BlockSpec((None, None, 1, d), lambda bi, i: (row(bi), chunk, 0, 0))


ADA_SHIFT1, ADA_SCALE1, ADA_GATE1, ADA_SHIFT2, ADA_SCALE2, ADA_GATE2 = range(6)


def _project(x, ada, cond_row, gain, w_in, q_gain_t, k_gain, bd, cos, sin, cos_t, sin_t, tm):
    b, s, d = x.shape
    in_w = w_in.shape[1]
    const = lambda shape: pl.BlockSpec(shape, lambda bi, i: (0,) * len(shape))
    row_groups = max(1, tm // MXU_COLS)
    return pl.pallas_call(
        functools.partial(_proj_kernel, row_groups=row_groups),
        grid=(b, s // tm),
        in_specs=[pl.BlockSpec((None, tm, d), lambda bi, i: (bi, i, 0)),
                  _ada_spec(ADA_SHIFT1, d, cond_row), _ada_spec(ADA_SCALE1, d, cond_row),
                  const((1, d)), const((d, in_w)),
                  const((HEAD_DIM, LANES)), const((1, KV_WIDTH)), const((KV_WIDTH, KV_WIDTH)),
                  pl.BlockSpec((tm, LANES), lambda bi, i: (i, 0)),
                  pl.BlockSpec((tm, LANES), lambda bi, i: (i, 0)),
                  pl.BlockSpec((HEAD_DIM, tm), lambda bi, i: (0, i)),
                  pl.BlockSpec((HEAD_DIM, tm), lambda bi, i: (0, i))],
        out_specs=[pl.BlockSpec((None, tm, FOURIER_WIDTH), lambda bi, i: (bi, i, 0)),
                   pl.BlockSpec((None, N_HEADS * SCORE_DEPTH, tm), lambda bi, i: (bi, 0, i)),
                   pl.BlockSpec((None, N_KV_HEADS, tm, SCORE_DEPTH), lambda bi, i: (bi, 0, i, 0)),
                   pl.BlockSpec((None, N_KV_HEADS * VT_ROWS, tm), lambda bi, i: (bi, 0, i))],
        out_shape=[jax.ShapeDtypeStruct((b, s, FOURIER_WIDTH), BF16),
                   jax.ShapeDtypeStruct((b, N_HEADS * SCORE_DEPTH, s), F8),
                   jax.ShapeDtypeStruct((b, N_KV_HEADS, s, SCORE_DEPTH), F8),
                   jax.ShapeDtypeStruct((b, N_KV_HEADS * VT_ROWS, s), BF16)],
        compiler_params=pltpu.CompilerParams(dimension_semantics=("parallel", "parallel"),
                                             vmem_limit_bytes=VMEM_LIMIT),
        name="proj",
    )(x, ada, ada, gain, w_in, q_gain_t, k_gain, bd, cos, sin, cos_t, sin_t)


def _fourier_kernel(u_ref, ma_ref, mb_ref, mc_ref, wf_ref, o_ref, zr_sc, zi_sc, tsc, *, unroll):
    n2_count = DFT_INNER
    n1_count = u_ref.shape[0] // n2_count
    c = FOURIER_GROUP_DIM
    t_rows = 2 * n1_count
    t_pitch = t_rows + SUBLANES
    o_pitch = n1_count + SUBLANES

    fw = _dot(mc_ref[...], wf_ref[...]).astype(BF16)
    fw_cat = jnp.concatenate([fw[:c], fw[c:]], axis=1)

    def fill(i, carry):
        src = pl.multiple_of(i * n2_count, n2_count)
        dst = pl.multiple_of(i * ROW_PITCH, SUBLANES)
        z = _dot(u_ref[pl.ds(src, n2_count), :], fw_cat)
        zr_sc[pl.ds(dst, n2_count), :] = z[:, :c]
        zi_sc[pl.ds(dst, n2_count), :] = z[:, c:]
        return carry
    lax.fori_loop(0, n1_count, fill, 0, unroll=min(unroll, n1_count))

    def stage_a(n2, carry):
        zr = zr_sc[pl.ds(n2, n1_count, stride=ROW_PITCH), :]
        zi = zi_sc[pl.ds(n2, n1_count, stride=ROW_PITCH), :]
        z2 = jnp.concatenate([zr, zi], axis=0).astype(BF16)
        dst = pl.multiple_of(n2 * t_pitch, SUBLANES)
        tsc[pl.ds(dst, t_rows), :] = _dot(ma_ref[n2], z2)
        return carry
    lax.fori_loop(0, n2_count, stage_a, 0, unroll=min(2 * unroll, n2_count))

    def stage_b(pair, carry):
        t2 = []
        for k1 in (2 * pair, 2 * pair + 1):
            tr = tsc[pl.ds(k1, n2_count, stride=t_pitch), :]
            ti = tsc[pl.ds(n1_count + k1, n2_count, stride=t_pitch), :]
            t2.append(jnp.concatenate([tr, ti], axis=0).astype(BF16))
        y = _dot(mb_ref[...], jnp.concatenate(t2, axis=1))
        o_ref[pl.ds(2 * pair, n2_count, stride=o_pitch), :] = y[:, :c]
        o_ref[pl.ds(2 * pair + 1, n2_count, stride=o_pitch), :] = y[:, c:]
        return carry
    lax.fori_loop(0, n1_count // 2, stage_b, 0, unroll=min(unroll, n1_count // 2))
    for pad_row in range(SUBLANES):
        o_ref[pl.ds(n1_count + pad_row, n2_count, stride=o_pitch), :] = jnp.zeros((n2_count, c), F32)


def _fourier(u, ma, mb, mc, w_four, unroll):
    b, s, _ = u.shape
    c = FOURIER_GROUP_DIM
    n1 = s // DFT_INNER
    single = pl.Buffered(1)
    return pl.pallas_call(
        functools.partial(_fourier_kernel, unroll=unroll),
        grid=(b, N_FOURIER_GROUPS),
        in_specs=[pl.BlockSpec((None, s, c), lambda bi, g: (bi, 0, g)),
                  pl.BlockSpec(ma.shape, lambda bi, g: (0, 0, 0), pipeline_mode=single),
                  pl.BlockSpec(mb.shape, lambda bi, g: (0, 0), pipeline_mode=single),
                  pl.BlockSpec(mc.shape, lambda bi, g: (0, 0), pipeline_mode=single),
                  pl.BlockSpec((None, c, c), lambda bi, g: (g, 0, 0))],
        out_specs=pl.BlockSpec((None, DFT_INNER * (n1 + SUBLANES), c), lambda bi, g: (bi, 0, g)),
        out_shape=jax.ShapeDtypeStruct((b, DFT_INNER * (n1 + SUBLANES), FOURIER_WIDTH), F32),
        scratch_shapes=[pltpu.VMEM((n1 * ROW_PITCH, c), F32),
                        pltpu.VMEM((n1 * ROW_PITCH, c), F32),
                        pltpu.VMEM((DFT_INNER * (2 * n1 + SUBLANES), c), F32)],
        compiler_params=pltpu.CompilerParams(dimension_semantics=("parallel", "parallel"),
                                             vmem_limit_bytes=VMEM_LIMIT),
        name="fourier",
    )(u, ma, mb, mc, w_four)


def _fourier_tables(s):
    n2c = DFT_INNER
    n1c = s // n2c
    k1 = np.arange(n1c, dtype=np.float64)
    n = (n2c * np.arange(n1c)[None, :] + np.arange(n2c)[:, None]).astype(np.float64)
    ang = 2.0 * np.pi * k1[None, :, None] * n[:, None, :] / s
    ca, sa = np.cos(ang), np.sin(ang)
    ma = np.concatenate([np.concatenate([ca, sa], axis=2),
                         np.concatenate([-sa, ca], axis=2)], axis=1)
    kk = np.arange(n2c, dtype=np.float64)
    a2 = 2.0 * np.pi * np.outer(kk, kk) / n2c
    mb = np.concatenate([np.cos(a2), np.sin(a2)], axis=1)
    cc = np.arange(FOURIER_GROUP_DIM, dtype=np.float64)
    ac = 2.0 * np.pi * np.outer(cc, cc) / FOURIER_GROUP_DIM
    norm = 1.0 / math.sqrt(s * FOURIER_GROUP_DIM)
    mc = np.concatenate([np.cos(ac), -np.sin(ac)], axis=0) * norm
    return (ma.astype(np.float32), mb.astype(np.float32), mc.astype(np.float32))


def _attn_kernel(scal_ref, qt_ref, k_ref, vt_ref, kc_ref, vtc_ref, o_ref, acc_ref, p_ref, *, tk,
                 group_blocks):
    tq = qt_ref.shape[1]
    lat_blocks = k_ref.shape[0] // tk
    n_blocks = lat_blocks + kc_ref.shape[0] // tk
    w = jnp.concatenate([qt_ref[h * SCORE_DEPTH:(h + 1) * SCORE_DEPTH, :] for h in range(Q_PER_KV)],
                        axis=1)
    rq = w.shape[1]
    acc_ref[...] = jnp.zeros(acc_ref.shape, F32)

    unshifted_ok = scal_ref[0] <= MAX_UNSHIFTED_SCORE

    def blocks(j):
        kr, vr, jj = (k_ref, vt_ref, j) if j < lat_blocks else (kc_ref, vtc_ref, j - lat_blocks)
        return kr[jj * tk:(jj + 1) * tk, :], vr[:, jj * tk:(jj + 1) * tk]

    @pl.when(unshifted_ok)
    def _():
        def group(gi):
            kvs = [blocks(gi * group_blocks + bi) for bi in range(group_blocks)]
            return [kb for kb, _ in kvs], jnp.concatenate([vb for _, vb in kvs], axis=1)

        kbs, _ = group(0)
        for bi, kb in enumerate(kbs):
            p_ref[bi * tk:(bi + 1) * tk, :] = jnp.exp2(_dot(kb, w)).astype(BF16)

        for gi in range(1, n_blocks // group_blocks):
            kbs, _ = group(gi)
            _, vb_prev = group(gi - 1)
            for c0 in range(0, rq, MXU_COLS):
                cols = slice(c0, c0 + MXU_COLS)
                acc_ref[:, cols] += _dot(vb_prev, p_ref[:, cols])
                for bi, kb in enumerate(kbs):
                    p_ref[bi * tk:(bi + 1) * tk, cols] = jnp.exp2(_dot(kb, w[:, cols])).astype(BF16)
        _, vb_last = group(n_blocks // group_blocks - 1)
        acc_ref[...] += _dot(vb_last, p_ref[...])

    @pl.when(jnp.logical_not(unshifted_ok))
    def _():
        score_scale = scal_ref[1]

        def update(kb, vb, m_prev):
            s = _dot(kb, w) * score_scale
            m_next = jnp.maximum(m_prev, jnp.max(s, axis=0, keepdims=True))
            alpha = jnp.exp2(m_prev - m_next)
            p = jnp.exp2(s - m_next).astype(BF16)
            acc_ref[...] = alpha * acc_ref[...] + _dot(vb, p)
            return m_next

        def latent_body(j, m_prev):
            start = pl.multiple_of(j * tk, tk)
            return update(k_ref[pl.ds(start, tk), :], vt_ref[:, pl.ds(start, tk)], m_prev)
        m = lax.fori_loop(0, lat_blocks, latent_body, jnp.full((1, rq), -jnp.inf, F32))
        for j in range(lat_blocks, n_blocks):
            m = update(*blocks(j), m)

    acc = acc_ref[...]
    out_t = acc[:HEAD_DIM] / acc[HEAD_DIM:HEAD_DIM + 1]
    out_t = jnp.concatenate([out_t[:, h * tq:(h + 1) * tq] for h in range(Q_PER_KV)], axis=0)
    o_ref[...] = out_t.T.astype(o_ref.dtype)


def _attention(qt, k, vt, kc, vtc, score_bound, score_scale, tq, tk):
    b, _, s = qt.shape
    n_ctx = kc.shape[2]
    assert s % tk == 0 and n_ctx % tk == 0
    n_blocks = (s + n_ctx) // tk
    group_blocks = next(g for g in (3, 2, 1) if n_blocks % g == 0)
    gw = Q_PER_KV * HEAD_DIM
    scal = jnp.stack([score_bound, score_scale]).astype(F32)
    return pl.pallas_call(
        functools.partial(_attn_kernel, tk=tk, group_blocks=group_blocks),
        grid=(b, N_KV_HEADS, s // tq),
        in_specs=[pl.BlockSpec(memory_space=pltpu.SMEM),
                  pl.BlockSpec((None, Q_PER_KV * SCORE_DEPTH, tq), lambda bi, g, i: (bi, g, i)),
                  pl.BlockSpec((None, None, s, SCORE_DEPTH), lambda bi, g, i: (bi, g, 0, 0)),
                  pl.BlockSpec((None, VT_ROWS, s), lambda bi, g, i: (bi, g, 0)),
                  pl.BlockSpec((None, None, n_ctx, SCORE_DEPTH), lambda bi, g, i: (bi, g, 0, 0)),
                  pl.BlockSpec((None, VT_ROWS, n_ctx), lambda bi, g, i: (bi, g, 0))],
        out_specs=pl.BlockSpec((None, tq, gw), lambda bi, g, i: (bi, i, g)),
        out_shape=jax.ShapeDtypeStruct((b, s, ATTN_WIDTH), BF16),
        scratch_shapes=[pltpu.VMEM((VT_ROWS, Q_PER_KV * tq), F32),
                        pltpu.VMEM((group_blocks * tk, Q_PER_KV * tq), BF16)],
        compiler_params=pltpu.CompilerParams(
            dimension_semantics=("parallel", "parallel", "parallel"),
            vmem_limit_bytes=VMEM_LIMIT),
        name="attention",
    )(scal, qt, k, vt, kc, vtc)


def _ffn_kernel(x_ref, fm_ref, at_ref, gt1_ref, sh2_ref, sc2_ref, gt2_ref, gffn_ref, gfin_ref,
                wo_ref, wg_ref, wu_ref, wd_ref, o_ref, *, ff_chunk, row_groups):
    tm = x_ref.shape[0]
    gm = tm // row_groups
    rows = [slice(i * gm, (i + 1) * gm) for i in range(row_groups)]
    n1 = fm_ref.shape[1] - SUBLANES
    k2g = gm // n1
    fms = [fm_ref[i * k2g:(i + 1) * k2g, :n1, :].reshape(gm, FOURIER_WIDTH) for i in range(row_groups)]
    mix = [_dot(f.astype(BF16), wo_ref[:FOURIER_WIDTH, :])
           + _dot(at_ref[r, :], wo_ref[FOURIER_WIDTH:, :]) for f, r in zip(fms, rows)]
    x1 = [x_ref[r, :] + gt1_ref[...] * m for r, m in zip(rows, mix)]
    h2 = [(_rms(x) * gffn_ref[...] * (1.0 + sc2_ref[...]) + sh2_ref[...]).astype(BF16) for x in x1]
    d_ff = wg_ref.shape[1]
    ffn = [None] * row_groups
    for c0 in range(0, d_ff, ff_chunk):
        gate = [_dot(h, wg_ref[:, c0:c0 + ff_chunk]) for h in h2]
        up = [_dot(h, wu_ref[:, c0:c0 + ff_chunk]) for h in h2]
        act = [(g / (1.0 + jnp.exp(-g)) * u).astype(BF16) for g, u in zip(gate, up)]
        for i, a in enumerate(act):
            part = _dot(a, wd_ref[c0:c0 + ff_chunk, :])
            ffn[i] = part if ffn[i] is None else ffn[i] + part
    for r, x, f in zip(rows, x1, ffn):
        o_ref[r, :] = _rms(x + gt2_ref[...] * f) * gfin_ref[...]


def _ffn(x, fm, at, ada, g_ffn, g_final, w_out, w_gate, w_up, w_down, tm, ff_chunk):
    b, s, d = x.shape
    d_ff = w_gate.shape[1]
    n1 = s // DFT_INNER
    fm = fm.reshape(b, DFT_INNER, n1 + SUBLANES, FOURIER_WIDTH)
    single = pl.Buffered(1)
    const = lambda shape: pl.BlockSpec(shape, lambda bi, i: (0,) * len(shape))
    weight = lambda shape: pl.BlockSpec(shape, lambda bi, i: (0,) * len(shape), pipeline_mode=single)
    return pl.pallas_call(
        functools.partial(_ffn_kernel, ff_chunk=ff_chunk, row_groups=max(1, tm // MXU_COLS)),
        grid=(b, s // tm),
        in_specs=[pl.BlockSpec((None, tm, d), lambda bi, i: (bi, i, 0)),
                  pl.BlockSpec((None, tm // n1, n1 + SUBLANES, FOURIER_WIDTH),
                               lambda bi, i: (bi, i, 0, 0)),
                  pl.BlockSpec((None, tm, ATTN_WIDTH), lambda bi, i: (bi, i, 0)),
                  _ada_spec(ADA_GATE1, d), _ada_spec(ADA_SHIFT2, d), _ada_spec(ADA_SCALE2, d),
                  _ada_spec(ADA_GATE2, d), const((1, d)), const((1, d)),
                  weight((d, d)), weight((d, d_ff)), weight((d, d_ff)), weight((d_ff, d))],
        out_specs=pl.BlockSpec((None, tm, d), lambda bi, i: (bi, i, 0)),
        out_shape=jax.ShapeDtypeStruct((b, s, d), F32),
        compiler_params=pltpu.CompilerParams(dimension_semantics=("parallel", "parallel"),
                                             vmem_limit_bytes=VMEM_LIMIT),
        name="ffn",
    )(x, fm, at, ada, ada, ada, ada, g_ffn, g_final, w_out, w_gate, w_up, w_down)


def _rope_tables(n_tokens):
    n_freq = HEAD_DIM // 4
    inv_freq = ROPE_THETA ** (-np.arange(n_freq, dtype=np.float64) / n_freq)
    t = np.arange(n_tokens)
    row_ang = (t // GRID_W).astype(np.float64)[:, None] * inv_freq[None, :]
    col_ang = (t % GRID_W).astype(np.float64)[:, None] * inv_freq[None, :]
    ang = np.concatenate([row_ang, row_ang, col_ang, col_ang], axis=1)
    sign = np.tile(np.concatenate([-np.ones(n_freq), np.ones(n_freq)]), 2)
    cos = np.cos(ang)
    sin = np.sin(ang) * sign[None, :]
    reps = LANES // HEAD_DIM
    f32 = lambda a: np.ascontiguousarray(a).astype(np.float32)
    return f32(np.tile(cos, (1, reps))), f32(np.tile(sin, (1, reps))), f32(cos.T), f32(sin.T)


def _head_block_diag(width):
    idx = np.arange(width) // HEAD_DIM
    return (idx[:, None] == idx[None, :]).astype(np.float32)


def kernel(x, c, ctx, c_ctx, w_ada, b_ada, g_mix, w_in, w_four, q_gain, k_gain, w_out, g_ffn,
           w_gate, w_up, w_down, g_final):
    b, s, d = x.shape
    n_ctx = ctx.shape[1]
    assert w_ada.shape[0] == 1, "single-layer block"
    assert s % (DFT_INNER * SUBLANES) == 0 and n_ctx % LANES == 0

    pad = (-(b + 1)) % SUBLANES
    cond = jnp.concatenate([c, c_ctx[None, :], jnp.zeros((pad, d), F32)], axis=0)
    ada = _adaln(cond, w_ada[0], b_ada[0][None, :]).reshape(b + 1 + pad, 6, 1, d)

    w_in_b = w_in[0].astype(BF16)
    bd = jnp.asarray(_head_block_diag(KV_WIDTH)).astype(BF16)
    gq_max = jnp.maximum(jnp.max(jnp.abs(q_gain[0])), jnp.finfo(F32).tiny)
    gk_max = jnp.maximum(jnp.max(jnp.abs(k_gain[0])), jnp.finfo(F32).tiny)
    score_scale = Q_SCALE * gq_max * gk_max
    score_bound = score_scale * (HEAD_DIM * SCORE_BOUND_SLACK)
    fold = score_bound <= MAX_UNSHIFTED_SCORE
    k_fold = jnp.where(fold, score_scale, 1.0)
    qg_t = jnp.broadcast_to((q_gain[0] / gq_max)[:, None], (HEAD_DIM, LANES))
    kg = jnp.tile(k_gain[0] * (k_fold / gk_max), N_KV_HEADS)[None, :]
    g_mix2 = g_mix[0][None, :]

    cos, sin, cos_t, sin_t = [jnp.asarray(t) for t in _rope_tables(s)]
    tm_proj = min(1024, s)
    u, qt, k, vt = _project(x, ada, None, g_mix2, w_in_b, qg_t, kg, bd,
                            cos, sin, cos_t, sin_t, tm_proj)
    _, _, kc, vtc = _project(ctx, ada, b, g_mix2, w_in_b, qg_t, kg, bd,
                             jnp.ones((n_ctx, LANES), F32), jnp.zeros((n_ctx, LANES), F32),
                             jnp.ones((HEAD_DIM, n_ctx), F32), jnp.zeros((HEAD_DIM, n_ctx), F32), n_ctx)
    ma, mb, mc = _fourier_tables(s)
    fm = _fourier(u, jnp.asarray(ma).astype(BF16), jnp.asarray(mb).astype(BF16),
                  jnp.asarray(mc), w_four[0], unroll=32)

    at = _attention(qt, k, vt, kc, vtc, score_bound, score_scale / k_fold, tq=min(512, s), tk=MXU_COLS)

    tm_ffn = min(512, s)
    d_ff = w_gate.shape[2]
    ff_chunk = d_ff
    return _ffn(x, fm, at, ada, g_ffn[0][None, :], g_final[None, :], w_out[0].astype(BF16),
                w_gate[0].astype(BF16), w_up[0].astype(BF16), w_down[0].astype(BF16), tm_ffn, ff_chunk)
```

```python
import functools
import math

import numpy as np
import jax
import jax.numpy as jnp
from jax import lax
from jax.experimental import pallas as pl
from jax.experimental.pallas import tpu as pltpu

F32 = jnp.float32
BF16 = jnp.bfloat16
F8 = jnp.float8_e4m3fn

EPS = 1e-6
ROPE_THETA = 10000.0
GRID_W = 64
HEAD_DIM = 64
N_HEADS = 8
N_KV_HEADS = 2
Q_PER_KV = N_HEADS // N_KV_HEADS
N_FOURIER_GROUPS = 4
FOURIER_GROUP_DIM = 128
FOURIER_WIDTH = N_FOURIER_GROUPS * FOURIER_GROUP_DIM
ATTN_WIDTH = N_HEADS * HEAD_DIM
KV_WIDTH = N_KV_HEADS * HEAD_DIM
SCORE_DEPTH = 4 * HEAD_DIM
BF16_SUBLANES = 16
VT_ROWS = HEAD_DIM + BF16_SUBLANES

LANES = 128
MXU_COLS = 256
SUBLANES = 8
DFT_INNER = 128
ROW_PITCH = DFT_INNER + SUBLANES
VMEM_LIMIT = 56 * 1024 * 1024

Q_SCALE = (HEAD_DIM ** -0.5) * math.log2(math.e)
SCORE_BOUND_SLACK = 1.02
MAX_UNSHIFTED_SCORE = 64.0


def _dot(a, b):
    return jnp.dot(a, b, preferred_element_type=F32)


def _split_bf16(a):
    hi = a.astype(BF16)
    return hi, (a - hi.astype(F32)).astype(BF16)


def _split_f8(a):
    hi = a.astype(F8)
    return hi, (a - hi.astype(F32)).astype(F8)


def _rms(x):
    return x * lax.rsqrt(jnp.mean(x * x, axis=-1, keepdims=True) + EPS)


def _adaln_kernel(cond_ref, w_ref, b_ref, o_ref):
    c = cond_ref[...]
    s_hi, s_lo = _split_bf16(c / (1.0 + jnp.exp(-c)))
    w_hi, w_lo = _split_bf16(w_ref[...])
    o_ref[...] = _dot(s_hi, w_hi) + _dot(s_lo, w_hi) + _dot(s_hi, w_lo) + b_ref[...]


def _adaln(cond, w, b):
    rows, d = cond.shape
    n = w.shape[1]
    tn = n // 4
    return pl.pallas_call(
        _adaln_kernel,
        grid=(n // tn,),
        in_specs=[pl.BlockSpec((rows, d), lambda j: (0, 0)),
                  pl.BlockSpec((d, tn), lambda j: (0, j)),
                  pl.BlockSpec((1, tn), lambda j: (0, j))],
        out_specs=pl.BlockSpec((rows, tn), lambda j: (0, j)),
        out_shape=jax.ShapeDtypeStruct((rows, n), F32),
        compiler_params=pltpu.CompilerParams(dimension_semantics=("arbitrary",),
                                             vmem_limit_bytes=VMEM_LIMIT),
        name="adaln",
    )(cond, w, b)


def _proj_kernel(x_ref, sh_ref, sc_ref, g_ref, w_ref, qgt_ref, kg_ref, bd_ref, cos_ref, sin_ref,
                 cost_ref, sint_ref, u_ref, qt_ref, k_ref, vt_ref, *, row_groups):
    tm = x_ref.shape[0]
    gm = tm // row_groups
    rows = [slice(i * gm, (i + 1) * gm) for i in range(row_groups)]
    o1 = FOURIER_WIDTH
    o2 = o1 + ATTN_WIDTH
    o3 = o2 + KV_WIDTH
    hs = [((_rms(x_ref[r, :]) * g_ref[...]) * (1.0 + sc_ref[...]) + sh_ref[...]).astype(BF16)
          for r in rows]
    ps = [_dot(h, w_ref[...]) for h in hs]

    lane = lax.broadcasted_iota(jnp.int32, (gm, LANES), 1)
    first_half = (lane & 31) < 16
    gain_t = jnp.tile(qgt_ref[...], (1, gm // LANES))
    quarter = HEAD_DIM // 4
    extra = VT_ROWS - HEAD_DIM
    ones_rows = jnp.where(lax.broadcasted_iota(jnp.int32, (extra, gm), 0) == 0, 1.0, 0.0).astype(vt_ref.dtype)

    for r, p in zip(rows, ps):
        u_ref[r, :] = p[:, :o1].astype(u_ref.dtype)
        v_t = p[:, o3:].T.astype(vt_ref.dtype)
        for kvh in range(N_KV_HEADS):
            vt_ref[kvh * VT_ROWS:kvh * VT_ROWS + HEAD_DIM, r] = v_t[kvh * HEAD_DIM:(kvh + 1) * HEAD_DIM]
            vt_ref[kvh * VT_ROWS + HEAD_DIM:(kvh + 1) * VT_ROWS, r] = ones_rows

        kx = p[:, o2:o3]
        hi, lo = _split_bf16(kx * kx)
        ss = _dot(hi, bd_ref[...]) + _dot(lo, bd_ref[...])
        kn = kx * lax.rsqrt(ss * (1.0 / HEAD_DIM) + EPS) * kg_ref[...]
        partner = jnp.where(first_half, pltpu.roll(kn, LANES - quarter, axis=1),
                            pltpu.roll(kn, quarter, axis=1))
        k_rot = kn * cos_ref[r, :] + partner * sin_ref[r, :]
        k_hi = k_rot.astype(F8).astype(F32)
        k_lo = k_rot - k_hi
        low_lanes = lane < HEAD_DIM
        per_head = (jnp.where(low_lanes, k_hi, pltpu.roll(k_lo, HEAD_DIM, axis=1)),
                    jnp.where(low_lanes, pltpu.roll(k_hi, HEAD_DIM, axis=1), k_lo))
        for kvh, kh in enumerate(per_head):
            kh8 = kh.astype(F8)
            k_ref[kvh, r, :LANES] = kh8
            k_ref[kvh, r, LANES:] = kh8

        cos_t = cost_ref[:, r]
        sin_t = sint_ref[:, r]
        for j in range(ATTN_WIDTH // LANES):
            qtj = p[:, o1 + j * LANES:o1 + (j + 1) * LANES].T
            for hh in range(LANES // HEAD_DIM):
                qh = qtj[hh * HEAD_DIM:(hh + 1) * HEAD_DIM]
                ss = jnp.sum(qh * qh, axis=0, keepdims=True)
                qn = qh * lax.rsqrt(ss * (1.0 / HEAD_DIM) + EPS) * gain_t
                partner = jnp.concatenate([qn[quarter:2 * quarter], qn[:quarter],
                                           qn[3 * quarter:], qn[2 * quarter:3 * quarter]], axis=0)
                q_hi, q_lo = _split_f8(qn * cos_t + partner * sin_t)
                row0 = (j * (LANES // HEAD_DIM) + hh) * SCORE_DEPTH
                for part, val in enumerate((q_hi, q_hi, q_lo, q_lo)):
                    qt_ref[row0 + part * HEAD_DIM:row0 + (part + 1) * HEAD_DIM, r] = val


def _ada_spec(chunk, d, cond_row=None):
    row = (lambda bi: bi) if cond_row is None else (lambda bi: cond_row)
    return pl.BlockSpec((None, None, 1, d), lambda bi, i: (row(bi), chunk, 0, 0))


ADA_SHIFT1, ADA_SCALE1, ADA_GATE1, ADA_SHIFT2, ADA_SCALE2, ADA_GATE2 = range(6)


def _project(x, ada, cond_row, gain, w_in, q_gain_t, k_gain, bd, cos, sin, cos_t, sin_t, tm):
    b, s, d = x.shape
    in_w = w_in.shape[1]
    const = lambda shape: pl.BlockSpec(shape, lambda bi, i: (0,) * len(shape))
    row_groups = max(1, tm // MXU_COLS)
    return pl.pallas_call(
        functools.partial(_proj_kernel, row_groups=row_groups),
        grid=(b, s // tm),
        in_specs=[pl.BlockSpec((None, tm, d), lambda bi, i: (bi, i, 0)),
                  _ada_spec(ADA_SHIFT1, d, cond_row), _ada_spec(ADA_SCALE1, d, cond_row),
                  const((1, d)), const((d, in_w)),
                  const((HEAD_DIM, LANES)), const((1, KV_WIDTH)), const((KV_WIDTH, KV_WIDTH)),
                  pl.BlockSpec((tm, LANES), lambda bi, i: (i, 0)),
                  pl.BlockSpec((tm, LANES), lambda bi, i: (i, 0)),
                  pl.BlockSpec((HEAD_DIM, tm), lambda bi, i: (0, i)),
                  pl.BlockSpec((HEAD_DIM, tm), lambda bi, i: (0, i))],
        out_specs=[pl.BlockSpec((None, tm, FOURIER_WIDTH), lambda bi, i: (bi, i, 0)),
                   pl.BlockSpec((None, N_HEADS * SCORE_DEPTH, tm), lambda bi, i: (bi, 0, i)),
                   pl.BlockSpec((None, N_KV_HEADS, tm, SCORE_DEPTH), lambda bi, i: (bi, 0, i, 0)),
                   pl.BlockSpec((None, N_KV_HEADS * VT_ROWS, tm), lambda bi, i: (bi, 0, i))],
        out_shape=[jax.ShapeDtypeStruct((b, s, FOURIER_WIDTH), BF16),
                   jax.ShapeDtypeStruct((b, N_HEADS * SCORE_DEPTH, s), F8),
                   jax.ShapeDtypeStruct((b, N_KV_HEADS, s, SCORE_DEPTH), F8),
                   jax.ShapeDtypeStruct((b, N_KV_HEADS * VT_ROWS, s), BF16)],
        compiler_params=pltpu.CompilerParams(dimension_semantics=("parallel", "parallel"),
                                             vmem_limit_bytes=VMEM_LIMIT),
        name="proj",
    )(x, ada, ada, gain, w_in, q_gain_t, k_gain, bd, cos, sin, cos_t, sin_t)


def _fourier_kernel(u_ref, ma_ref, mb_ref, mc_ref, wf_ref, o_ref, zr_sc, zi_sc, tsc, *, unroll):
    n2_count = DFT_INNER
    n1_count = u_ref.shape[0] // n2_count
    c = FOURIER_GROUP_DIM
    t_rows = 2 * n1_count
    t_pitch = t_rows + SUBLANES
    o_pitch = n1_count + SUBLANES

    fw = _dot(mc_ref[...], wf_ref[...]).astype(BF16)
    fw_cat = jnp.concatenate([fw[:c], fw[c:]], axis=1)

    def fill(i, carry):
        src = pl.multiple_of(i * n2_count, n2_count)
        dst = pl.multiple_of(i * ROW_PITCH, SUBLANES)
        z = _dot(u_ref[pl.ds(src, n2_count), :], fw_cat)
        zr_sc[pl.ds(dst, n2_count), :] = z[:, :c]
        zi_sc[pl.ds(dst, n2_count), :] = z[:, c:]
        return carry
    lax.fori_loop(0, n1_count, fill, 0, unroll=min(unroll, n1_count))

    def stage_a(n2, carry):
        zr = zr_sc[pl.ds(n2, n1_count, stride=ROW_PITCH), :]
        zi = zi_sc[pl.ds(n2, n1_count, stride=ROW_PITCH), :]
        z2 = jnp.concatenate([zr, zi], axis=0).astype(BF16)
        dst = pl.multiple_of(n2 * t_pitch, SUBLANES)
        tsc[pl.ds(dst, t_rows), :] = _dot(ma_ref[n2], z2)
        return carry
    lax.fori_loop(0, n2_count, stage_a, 0, unroll=min(2 * unroll, n2_count))

    def stage_b(pair, carry):
        t2 = []
        for k1 in (2 * pair, 2 * pair + 1):
            tr = tsc[pl.ds(k1, n2_count, stride=t_pitch), :]
            ti = tsc[pl.ds(n1_count + k1, n2_count, stride=t_pitch), :]
            t2.append(jnp.concatenate([tr, ti], axis=0).astype(BF16))
        y = _dot(mb_ref[...], jnp.concatenate(t2, axis=1))
        o_ref[pl.ds(2 * pair, n2_count, stride=o_pitch), :] = y[:, :c]
        o_ref[pl.ds(2 * pair + 1, n2_count, stride=o_pitch), :] = y[:, c:]
        return carry
    lax.fori_loop(0, n1_count // 2, stage_b, 0, unroll=min(unroll, n1_count // 2))
    for pad_row in range(SUBLANES):
        o_ref[pl.ds(n1_count + pad_row, n2_count, stride=o_pitch), :] = jnp.zeros((n2_count, c), F32)


def _fourier(u, ma, mb, mc, w_four, unroll):
    b, s, _ = u.shape
    c = FOURIER_GROUP_DIM
    n1 = s // DFT_INNER
    single = pl.Buffered(1)
    return pl.pallas_call(
        functools.partial(_fourier_kernel, unroll=unroll),
        grid=(b, N_FOURIER_GROUPS),
        in_specs=[pl.BlockSpec((None, s, c), lambda bi, g: (bi, 0, g)),
                  pl.BlockSpec(ma.shape, lambda bi, g: (0, 0, 0), pipeline_mode=single),
                  pl.BlockSpec(mb.shape, lambda bi, g: (0, 0), pipeline_mode=single),
                  pl.BlockSpec(mc.shape, lambda bi, g: (0, 0), pipeline_mode=single),
                  pl.BlockSpec((None, c, c), lambda bi, g: (g, 0, 0))],
        out_specs=pl.BlockSpec((None, DFT_INNER * (n1 + SUBLANES), c), lambda bi, g: (bi, 0, g)),
        out_shape=jax.ShapeDtypeStruct((b, DFT_INNER * (n1 + SUBLANES), FOURIER_WIDTH), F32),
        scratch_shapes=[pltpu.VMEM((n1 * ROW_PITCH, c), F32),
                        pltpu.VMEM((n1 * ROW_PITCH, c), F32),
                        pltpu.VMEM((DFT_INNER * (2 * n1 + SUBLANES), c), F32)],
        compiler_params=pltpu.CompilerParams(dimension_semantics=("parallel", "parallel"),
                                             vmem_limit_bytes=VMEM_LIMIT),
        name="fourier",
    )(u, ma, mb, mc, w_four)


def _fourier_tables(s):
    n2c = DFT_INNER
    n1c = s // n2c
    k1 = np.arange(n1c, dtype=np.float64)
    n = (n2c * np.arange(n1c)[None, :] + np.arange(n2c)[:, None]).astype(np.float64)
    ang = 2.0 * np.pi * k1[None, :, None] * n[:, None, :] / s
    ca, sa = np.cos(ang), np.sin(ang)
    ma = np.concatenate([np.concatenate([ca, sa], axis=2),
                         np.concatenate([-sa, ca], axis=2)], axis=1)
    kk = np.arange(n2c, dtype=np.float64)
    a2 = 2.0 * np.pi * np.outer(kk, kk) / n2c
    mb = np.concatenate([np.cos(a2), np.sin(a2)], axis=1)
    cc = np.arange(FOURIER_GROUP_DIM, dtype=np.float64)
    ac = 2.0 * np.pi * np.outer(cc, cc) / FOURIER_GROUP_DIM
    norm = 1.0 / math.sqrt(s * FOURIER_GROUP_DIM)
    mc = np.concatenate([np.cos(ac), -np.sin(ac)], axis=0) * norm
    return (ma.astype(np.float32), mb.astype(np.float32), mc.astype(np.float32))


def _attn_kernel(scal_ref, qt_ref, k_ref, vt_ref, kc_ref, vtc_ref, o_ref, acc_ref, p_ref, *, tk,
                 group_blocks):
    tq = qt_ref.shape[1]
    lat_blocks = k_ref.shape[0] // tk
    n_blocks = lat_blocks + kc_ref.shape[0] // tk
    w = jnp.concatenate([qt_ref[h * SCORE_DEPTH:(h + 1) * SCORE_DEPTH, :] for h in range(Q_PER_KV)],
                        axis=1)
    rq = w.shape[1]
    step = pl.program_id(2)
    n_tiles = pl.num_programs(2) - 1
    cur = step % 2
    prev = 1 - cur

    @pl.when(step == 0)
    def _():
        acc_ref[...] = jnp.ones(acc_ref.shape, F32)
        p_ref[...] = jnp.zeros(p_ref.shape, BF16)

    unshifted_ok = scal_ref[0] <= MAX_UNSHIFTED_SCORE

    def blocks(j):
        kr, vr, jj = (k_ref, vt_ref, j) if j < lat_blocks else (kc_ref, vtc_ref, j - lat_blocks)
        return kr[jj * tk:(jj + 1) * tk, :], vr[:, jj * tk:(jj + 1) * tk]

    def group(gi):
        kvs = [blocks(gi * group_blocks + bi) for bi in range(group_blocks)]
        return [kb for kb, _ in kvs], jnp.concatenate([vb for _, vb in kvs], axis=1)

    n_groups = n_blocks // group_blocks

    def finish_previous():
        acc = acc_ref[prev]
        out_t = acc[:HEAD_DIM] / acc[HEAD_DIM:HEAD_DIM + 1]
        out_t = jnp.concatenate([out_t[:, h * tq:(h + 1) * tq] for h in range(Q_PER_KV)], axis=0)
        o_ref[...] = out_t.T.astype(o_ref.dtype)

    @pl.when(jnp.logical_and(unshifted_ok, step < n_tiles))
    def _():
        for gi in range(n_groups):
            kbs, _ = group(gi)
            _, vb_prev = group((gi - 1) % n_groups)
            slot = prev if gi == 0 else cur
            for c0 in range(0, rq, MXU_COLS):
                cols = slice(c0, c0 + MXU_COLS)
                acc_ref[slot, :, cols] += _dot(vb_prev, p_ref[:, cols])
                for bi, kb in enumerate(kbs):
                    p_ref[bi * tk:(bi + 1) * tk, cols] = jnp.exp2(_dot(kb, w[:, cols])).astype(BF16)
            if gi == 0:
                finish_previous()
                acc_ref[cur] = jnp.zeros(acc_ref.shape[1:], F32)

    @pl.when(jnp.logical_and(jnp.logical_not(unshifted_ok), step < n_tiles))
    def _():
        score_scale = scal_ref[1]
        finish_previous()
        acc_ref[cur] = jnp.zeros(acc_ref.shape[1:], F32)

        def update(kb, vb, m_prev):
            s = _dot(kb, w) * score_scale
            m_next = jnp.maximum(m_prev, jnp.max(s, axis=0, keepdims=True))
            alpha = jnp.exp2(m_prev - m_next)
            p = jnp.exp2(s - m_next).astype(BF16)
            acc_ref[cur] = alpha * acc_ref[cur] + _dot(vb, p)
            return m_next

        def latent_body(j, m_prev):
            start = pl.multiple_of(j * tk, tk)
            return update(k_ref[pl.ds(start, tk), :], vt_ref[:, pl.ds(start, tk)], m_prev)
        m = lax.fori_loop(0, lat_blocks, latent_body, jnp.full((1, rq), -jnp.inf, F32))
        for j in range(lat_blocks, n_blocks):
            m = update(*blocks(j), m)

    @pl.when(step == n_tiles)
    def _():
        @pl.when(unshifted_ok)
        def _():
            acc_ref[prev] += _dot(group(n_groups - 1)[1], p_ref[...])
        finish_previous()


def _attention(qt, k, vt, kc, vtc, score_bound, score_scale, tq, tk):
    b, _, s = qt.shape
    n_ctx = kc.shape[2]
    assert s % tk == 0 and n_ctx % tk == 0
    n_tiles = s // tq
    n_blocks = (s + n_ctx) // tk
    group_blocks = next(g for g in (3, 2, 1) if n_blocks % g == 0)
    gw = Q_PER_KV * HEAD_DIM
    scal = jnp.stack([score_bound, score_scale]).astype(F32)
    return pl.pallas_call(
        functools.partial(_attn_kernel, tk=tk, group_blocks=group_blocks),
        grid=(b, N_KV_HEADS, n_tiles + 1),
        in_specs=[pl.BlockSpec(memory_space=pltpu.SMEM),
                  pl.BlockSpec((None, Q_PER_KV * SCORE_DEPTH, tq),
                               lambda bi, g, i: (bi, g, jnp.minimum(i, n_tiles - 1))),
                  pl.BlockSpec((None, None, s, SCORE_DEPTH), lambda bi, g, i: (bi, g, 0, 0)),
                  pl.BlockSpec((None, VT_ROWS, s), lambda bi, g, i: (bi, g, 0)),
                  pl.BlockSpec((None, None, n_ctx, SCORE_DEPTH), lambda bi, g, i: (bi, g, 0, 0)),
                  pl.BlockSpec((None, VT_ROWS, n_ctx), lambda bi, g, i: (bi, g, 0))],
        out_specs=pl.BlockSpec((None, tq, gw), lambda bi, g, i: (bi, jnp.maximum(i - 1, 0), g)),
        out_shape=jax.ShapeDtypeStruct((b, s, ATTN_WIDTH), BF16),
        scratch_shapes=[pltpu.VMEM((2, VT_ROWS, Q_PER_KV * tq), F32),
                        pltpu.VMEM((group_blocks * tk, Q_PER_KV * tq), BF16)],
        compiler_params=pltpu.CompilerParams(
            dimension_semantics=("parallel", "parallel", "arbitrary"),
            vmem_limit_bytes=VMEM_LIMIT),
        name="attention",
    )(scal, qt, k, vt, kc, vtc)


def _ffn_kernel(x_ref, fm_ref, at_ref, gt1_ref, sh2_ref, sc2_ref, gt2_ref, gffn_ref, gfin_ref,
                wo_ref, wg_ref, wu_ref, wd_ref, o_ref, *, ff_chunk, row_groups):
    tm = x_ref.shape[0]
    gm = tm // row_groups
    rows = [slice(i * gm, (i + 1) * gm) for i in range(row_groups)]
    n1 = fm_ref.shape[1] - SUBLANES
    k2g = gm // n1
    fms = [fm_ref[i * k2g:(i + 1) * k2g, :n1, :].reshape(gm, FOURIER_WIDTH) for i in range(row_groups)]
    mix = [_dot(f.astype(BF16), wo_ref[:FOURIER_WIDTH, :])
           + _dot(at_ref[r, :], wo_ref[FOURIER_WIDTH:, :]) for f, r in zip(fms, rows)]
    x1 = [x_ref[r, :] + gt1_ref[...] * m for r, m in zip(rows, mix)]
    h2 = [(_rms(x) * gffn_ref[...] * (1.0 + sc2_ref[...]) + sh2_ref[...]).astype(BF16) for x in x1]
    d_ff = wg_ref.shape[1]
    ffn = [None] * row_groups
    for c0 in range(0, d_ff, ff_chunk):
        gate = [_dot(h, wg_ref[:, c0:c0 + ff_chunk]) for h in h2]
        up = [_dot(h, wu_ref[:, c0:c0 + ff_chunk]) for h in h2]
        act = [(g / (1.0 + jnp.exp(-g)) * u).astype(BF16) for g, u in zip(gate, up)]
        for i, a in enumerate(act):
            part = _dot(a, wd_ref[c0:c0 + ff_chunk, :])
            ffn[i] = part if ffn[i] is None else ffn[i] + part
    for r, x, f in zip(rows, x1, ffn):
        o_ref[r, :] = _rms(x + gt2_ref[...] * f) * gfin_ref[...]


def _ffn(x, fm, at, ada, g_ffn, g_final, w_out, w_gate, w_up, w_down, tm, ff_chunk):
    b, s, d = x.shape
    d_ff = w_gate.shape[1]
    n1 = s // DFT_INNER
    fm = fm.reshape(b, DFT_INNER, n1 + SUBLANES, FOURIER_WIDTH)
    single = pl.Buffered(1)
    const = lambda shape: pl.BlockSpec(shape, lambda bi, i: (0,) * len(shape))
    weight = lambda shape: pl.BlockSpec(shape, lambda bi, i: (0,) * len(shape), pipeline_mode=single)
    return pl.pallas_call(
        functools.partial(_ffn_kernel, ff_chunk=ff_chunk, row_groups=max(1, tm // MXU_COLS)),
        grid=(b, s // tm),
        in_specs=[pl.BlockSpec((None, tm, d), lambda bi, i: (bi, i, 0)),
                  pl.BlockSpec((None, tm // n1, n1 + SUBLANES, FOURIER_WIDTH),
                               lambda bi, i: (bi, i, 0, 0)),
                  pl.BlockSpec((None, tm, ATTN_WIDTH), lambda bi, i: (bi, i, 0)),
                  _ada_spec(ADA_GATE1, d), _ada_spec(ADA_SHIFT2, d), _ada_spec(ADA_SCALE2, d),
                  _ada_spec(ADA_GATE2, d), const((1, d)), const((1, d)),
                  weight((d, d)), weight((d, d_ff)), weight((d, d_ff)), weight((d_ff, d))],
        out_specs=pl.BlockSpec((None, tm, d), lambda bi, i: (bi, i, 0)),
        out_shape=jax.ShapeDtypeStruct((b, s, d), F32),
        compiler_params=pltpu.CompilerParams(dimension_semantics=("parallel", "parallel"),
                                             vmem_limit_bytes=VMEM_LIMIT),
        name="ffn",
    )(x, fm, at, ada, ada, ada, ada, g_ffn, g_final, w_out, w_gate, w_up, w_down)


def _rope_tables(n_tokens):
    n_freq = HEAD_DIM // 4
    inv_freq = ROPE_THETA ** (-np.arange(n_freq, dtype=np.float64) / n_freq)
    t = np.arange(n_tokens)
    row_ang = (t // GRID_W).astype(np.float64)[:, None] * inv_freq[None, :]
    col_ang = (t % GRID_W).astype(np.float64)[:, None] * inv_freq[None, :]
    ang = np.concatenate([row_ang, row_ang, col_ang, col_ang], axis=1)
    sign = np.tile(np.concatenate([-np.ones(n_freq), np.ones(n_freq)]), 2)
    cos = np.cos(ang)
    sin = np.sin(ang) * sign[None, :]
    reps = LANES // HEAD_DIM
    f32 = lambda a: np.ascontiguousarray(a).astype(np.float32)
    return f32(np.tile(cos, (1, reps))), f32(np.tile(sin, (1, reps))), f32(cos.T), f32(sin.T)


def _head_block_diag(width):
    idx = np.arange(width) // HEAD_DIM
    return (idx[:, None] == idx[None, :]).astype(np.float32)


def kernel(x, c, ctx, c_ctx, w_ada, b_ada, g_mix, w_in, w_four, q_gain, k_gain, w_out, g_ffn,
           w_gate, w_up, w_down, g_final):
    b, s, d = x.shape
    n_ctx = ctx.shape[1]
    assert w_ada.shape[0] == 1, "single-layer block"
    assert s % (DFT_INNER * SUBLANES) == 0 and n_ctx % LANES == 0

    pad = (-(b + 1)) % SUBLANES
    cond = jnp.concatenate([c, c_ctx[None, :], jnp.zeros((pad, d), F32)], axis=0)
    ada = _adaln(cond, w_ada[0], b_ada[0][None, :]).reshape(b + 1 + pad, 6, 1, d)

    w_in_b = w_in[0].astype(BF16)
    bd = jnp.asarray(_head_block_diag(KV_WIDTH)).astype(BF16)
    gq_max = jnp.maximum(jnp.max(jnp.abs(q_gain[0])), jnp.finfo(F32).tiny)
    gk_max = jnp.maximum(jnp.max(jnp.abs(k_gain[0])), jnp.finfo(F32).tiny)
    score_scale = Q_SCALE * gq_max * gk_max
    score_bound = score_scale * (HEAD_DIM * SCORE_BOUND_SLACK)
    fold = score_bound <= MAX_UNSHIFTED_SCORE
    k_fold = jnp.where(fold, score_scale, 1.0)
    qg_t = jnp.broadcast_to((q_gain[0] / gq_max)[:, None], (HEAD_DIM, LANES))
    kg = jnp.tile(k_gain[0] * (k_fold / gk_max), N_KV_HEADS)[None, :]
    g_mix2 = g_mix[0][None, :]

    cos, sin, cos_t, sin_t = [jnp.asarray(t) for t in _rope_tables(s)]
    tm_proj = min(1024, s)
    u, qt, k, vt = _project(x, ada, None, g_mix2, w_in_b, qg_t, kg, bd,
                            cos, sin, cos_t, sin_t, tm_proj)
    _, _, kc, vtc = _project(ctx, ada, b, g_mix2, w_in_b, qg_t, kg, bd,
                             jnp.ones((n_ctx, LANES), F32), jnp.zeros((n_ctx, LANES), F32),
                             jnp.ones((HEAD_DIM, n_ctx), F32), jnp.zeros((HEAD_DIM, n_ctx), F32), n_ctx)
    ma, mb, mc = _fourier_tables(s)
    fm = _fourier(u, jnp.asarray(ma).astype(BF16), jnp.asarray(mb).astype(BF16),
                  jnp.asarray(mc), w_four[0], unroll=32)

    at = _attention(qt, k, vt, kc, vtc, score_bound, score_scale / k_fold, tq=min(512, s), tk=MXU_COLS)

    tm_ffn = min(512, s)
    d_ff = w_gate.shape[2]
    ff_chunk = d_ff
    return _ffn(x, fm, at, ada, g_ffn[0][None, :], g_final[None, :], w_out[0].astype(BF16),
                w_gate[0].astype(BF16), w_up[0].astype(BF16), w_down[0].astype(BF16), tm_ffn, ff_chunk)
```
